```python
import jax, jax.numpy as jnp
from jax import lax
import numpy as np

D_MODEL = 4096
BATCH = 1
SEQ = 8192
DEPTH = 2

D_MIX = D_MODEL
D_A = D_MIX // 2
D_B = D_MIX - D_A
A_HEAD_DIM = 128
A_HEADS = D_A // A_HEAD_DIM
A_KV_HEADS = A_HEADS // 4
IDX_HEADS = 16
IDX_DIM = 64
TOPK_MAX = 256
Q_BLOCK = 128
B_HEAD_DIM = 64
B_HEADS = D_B // B_HEAD_DIM
B_KV_HEADS = B_HEADS // 8
WINDOW = 128
D_FF = 256 * ((8 * D_MODEL // 3 + 255) // 256)
CONV_WIDTH = 3
ROPE_THETA = 10000.0
EPS = 1e-6
NEG = -1e30
N_MOD = 6

SPLITS = (D_A, A_KV_HEADS * A_HEAD_DIM, A_KV_HEADS * A_HEAD_DIM,
          IDX_HEADS * IDX_DIM, IDX_DIM, IDX_HEADS,
          D_B, B_KV_HEADS * B_HEAD_DIM, B_KV_HEADS * B_HEAD_DIM)
N_IN = sum(SPLITS)

kernel_name = "hybrid_dsa_swasink_convffn_adaln"


def rms_norm(x, g):
    xf = x.astype(jnp.float32)
    y = xf * lax.rsqrt(jnp.mean(xf * xf, axis=-1, keepdims=True) + EPS)
    return (y * g.astype(jnp.float32)).astype(x.dtype)


def rope_tables(positions, dim):
    inv_freq = ROPE_THETA ** (-jnp.arange(0, dim, 2, dtype=jnp.float32) / dim)
    ang = positions.astype(jnp.float32)[..., None] * inv_freq
    return jnp.cos(ang)[:, :, None, :], jnp.sin(ang)[:, :, None, :]


def apply_rope(x, cos, sin):
    xf = x.astype(jnp.float32)
    x1, x2 = jnp.split(xf, 2, axis=-1)
    return jnp.concatenate([x1 * cos - x2 * sin, x2 * cos + x1 * sin], axis=-1).astype(x.dtype)


def dsa_attention(q, k, v, q_idx, k_idx, w_idx):
    b, s, hq, d = q.shape
    hkv = k.shape[2]
    grp = hq // hkv
    n_sel = min(TOPK_MAX, s // 4)
    nb = s // Q_BLOCK
    key_pos = jnp.arange(s)
    bidx = jnp.arange(b)[:, None, None]
    k_idx_f = k_idx.astype(jnp.float32)
    w_scaled = w_idx.astype(jnp.float32) * (IDX_HEADS ** -0.5)

    def to_blocks(t):
        return jnp.moveaxis(t.reshape((b, nb, Q_BLOCK) + t.shape[2:]), 1, 0)

    def block(args):
        qb, qib, wb, start = args
        logits = jnp.einsum('bqhd,bsd->bqhs', qib.astype(jnp.float32), k_idx_f) * (IDX_DIM ** -0.5)
        score = jnp.einsum('bqh,bqhs->bqs', wb, jax.nn.relu(logits))
        q_pos = start + jnp.arange(Q_BLOCK)
        causal = key_pos[None, :] <= q_pos[:, None]
        score = jnp.where(causal[None], score, NEG)
        _, sel = lax.top_k(score, n_sel)
        valid = sel <= q_pos[None, :, None]
        ks = k[bidx, sel].astype(jnp.float32)
        vs = v[bidx, sel].astype(jnp.float32)
        qg = qb.reshape(b, Q_BLOCK, hkv, grp, d).astype(jnp.float32)
        att = jnp.einsum('bqgrd,bqkgd->bqgrk', qg, ks) * (d ** -0.5)
        att = jnp.where(valid[:, :, None, None, :], att, NEG)
        p = jax.nn.softmax(att, axis=-1)
        o = jnp.einsum('bqgrk,bqkgd->bqgrd', p, vs)
        return o.reshape(b, Q_BLOCK, hq * d).astype(q.dtype)

    starts = jnp.arange(nb) * Q_BLOCK
    out = lax.map(block, (to_blocks(q), to_blocks(q_idx), to_blocks(w_scaled), starts))
    return jnp.moveaxis(out, 0, 1).reshape(b, s, hq * d)


def swa_sink_attention(q, k, v, sinks):
    b, s, hq, d = q.shape
    hkv = k.shape[2]
    grp = hq // hkv
    nb = s // WINDOW
    qb = q.reshape(b, nb, WINDOW, hkv, grp, d).astype(jnp.float32)

    def with_prev(t):
        tb = t.reshape(b, nb, WINDOW, hkv, d).astype(jnp.float32)
        prev = jnp.pad(tb, ((0, 0), (1, 0), (0, 0), (0, 0), (0, 0)))[:, :-1]
        return jnp.concatenate([prev, tb], axis=2)

    kk, vv = with_prev(k), with_prev(v)
    logits = jnp.einsum('bnqgrd,bnkgd->bngrqk', qb, kk) * (d ** -0.5)
    qi = jnp.arange(WINDOW)[:, None]
    ki = jnp.arange(2 * WINDOW)[None, :] - WINDOW
    band = (ki <= qi) & (qi - ki < WINDOW)
    in_seq = (jnp.arange(nb)[:, None, None] * WINDOW + ki[None]) >= 0
    mask = band[None] & in_seq
    logits = jnp.where(mask[None, :, None, None], logits, NEG)
    sink = jnp.broadcast_to(sinks.astype(jnp.float32).reshape(1, 1, hkv, grp, 1, 1),
                            logits.shape[:-1] + (1,))
    p = jax.nn.softmax(jnp.concatenate([logits, sink], axis=-1), axis=-1)[..., :-1]
    o = jnp.einsum('bngrqk,bnkgd->bnqgrd', p, vv)
    return o.reshape(b, s, hq * d).astype(q.dtype)


def conv_ffn(h, w_up, conv_w, conv_b, w_down):
    u = h @ w_up
    ch = u.shape[-1]
    u = lax.conv_general_dilated(u, conv_w[:, None, :].astype(u.dtype), window_strides=(1,),
                                 padding=[(CONV_WIDTH - 1, 0)],
                                 dimension_numbers=('NWC', 'WIO', 'NWC'),
                                 feature_group_count=ch) + conv_b
    gate, val = jnp.split(u, 2, axis=-1)
    return (jax.nn.silu(gate) * val) @ w_down


def setup_inputs(seed: int = 0) -> dict:
    key = jax.random.key(seed)
    ks = jax.random.split(key, 20)
    f32 = jnp.float32
    nrm = lambda k, shape, scale: jax.random.normal(k, shape, f32) * scale
    gain = lambda k, shape: 1.0 + 0.05 * jax.random.normal(k, shape, f32)
    offset = jax.random.randint(ks[2], (BATCH, 1), 0, 1024, dtype=jnp.int32)
    return {
        "x": nrm(ks[0], (BATCH, SEQ, D_MODEL), 1.0),
        "c": nrm(ks[1], (BATCH, D_MODEL), 1.0),
        "positions": offset + jnp.arange(SEQ, dtype=jnp.int32)[None, :],
        "w_ada": nrm(ks[3], (DEPTH, D_MODEL, N_MOD * D_MODEL), 0.5 * D_MODEL ** -0.5),
        "b_ada": nrm(ks[4], (DEPTH, N_MOD * D_MODEL), 0.01),
        "g_mix": gain(ks[5], (DEPTH, D_MODEL)),
        "w_in": nrm(ks[6], (DEPTH, D_MODEL, N_IN), D_MODEL ** -0.5),
        "g_out_a": gain(ks[7], (DEPTH, D_A)),
        "g_out_b": gain(ks[8], (DEPTH, D_B)),
        "sinks": nrm(ks[9], (DEPTH, B_HEADS), 0.5),
        "w_out": nrm(ks[10], (DEPTH, D_MIX, D_MODEL), D_MIX ** -0.5),
        "g_ffn": gain(ks[11], (DEPTH, D_MODEL)),
        "w_up": nrm(ks[12], (DEPTH, D_MODEL, 2 * D_FF), D_MODEL ** -0.5),
        "conv_w": nrm(ks[13], (DEPTH, CONV_WIDTH, 2 * D_FF), CONV_WIDTH ** -0.5),
        "conv_b": nrm(ks[14], (DEPTH, 2 * D_FF), 0.01),
        "w_down": nrm(ks[15], (DEPTH, D_FF, D_MODEL), D_FF ** -0.5),
        "g_final": gain(ks[16], (D_MODEL,)),
    }


def reference(x, c, positions, w_ada, b_ada, g_mix, w_in, g_out_a, g_out_b, sinks, w_out,
              g_ffn, w_up, conv_w, conv_b, w_down, g_final):
    b, s, _ = x.shape
    cos_a, sin_a = rope_tables(positions, A_HEAD_DIM)
    cos_b, sin_b = rope_tables(positions, B_HEAD_DIM)
    cos_i, sin_i = rope_tables(positions, IDX_DIM)
    split_points = []
    acc = 0
    for width in SPLITS[:-1]:
        acc += width
        split_points.append(acc)
    c_act = jax.nn.silu(c)
    for l in range(DEPTH):
        mod = c_act @ w_ada[l] + b_ada[l]
        sh_m, sc_m, gt_m, sh_f, sc_f, gt_f = [m[:, None, :] for m in jnp.split(mod, N_MOD, axis=-1)]
        h = rms_norm(x, g_mix[l]) * (1.0 + sc_m) + sh_m
        proj = h @ w_in[l]
        qa, ka, va, qi, ki, wi, qb, kb, vb = jnp.split(proj, split_points, axis=-1)
        qa = apply_rope(qa.reshape(b, s, A_HEADS, A_HEAD_DIM), cos_a, sin_a)
        ka = apply_rope(ka.reshape(b, s, A_KV_HEADS, A_HEAD_DIM), cos_a, sin_a)
        va = va.reshape(b, s, A_KV_HEADS, A_HEAD_DIM)
        qi = apply_rope(qi.reshape(b, s, IDX_HEADS, IDX_DIM), cos_i, sin_i)
        ki = apply_rope(ki.reshape(b, s, 1, IDX_DIM), cos_i, sin_i)[:, :, 0]
        qb = apply_rope(qb.reshape(b, s, B_HEADS, B_HEAD_DIM), cos_b, sin_b)
        kb = apply_rope(kb.reshape(b, s, B_KV_HEADS, B_HEAD_DIM), cos_b, sin_b)
        vb = vb.reshape(b, s, B_KV_HEADS, B_HEAD_DIM)
        out_a = dsa_attention(qa, ka, va, qi, ki, wi)
        out_b = swa_sink_attention(qb, kb, vb, sinks[l])
        mixed = jnp.concatenate([rms_norm(out_a, g_out_a[l]), rms_norm(out_b, g_out_b[l])],
                                axis=-1) @ w_out[l]
        x = x + gt_m * mixed
        h = rms_norm(x, g_ffn[l]) * (1.0 + sc_f) + sh_f
        x = x + gt_f * conv_ffn(h, w_up[l], conv_w[l], conv_b[l], w_down[l])
    return rms_norm(x, g_final)
```

```python
import functools
import math

import numpy as np
import jax
import jax.numpy as jnp
from jax import lax
from jax.experimental import pallas as pl
from jax.experimental.pallas import tpu as pltpu

A_HEAD_DIM = 128
A_GROUP = 4
IDX_HEADS = 16
IDX_DIM = 64
TOPK_MAX = 256
B_HEAD_DIM = 64
B_GROUP = 8
WINDOW = 128
CONV_WIDTH = 3
ROPE_THETA = 10000.0
EPS = 1e-6
NEG = -1e30
M_INIT = -1e29
N_MOD = 6
LOG2E = 1.4426950408889634

LANES = 128
VMEM_LIMIT = 56 * 1024 * 1024

TQ = 128
KC = 512

F32 = jnp.float32
BF16 = jnp.bfloat16
I32 = jnp.int32

_NEG_BITS = int(np.float32(NEG).view(np.int32))
KEY_NEG = _NEG_BITS ^ ((_NEG_BITS >> 31) & 0x7FFFFFFF)
INT_MIN = -(2 ** 31)


def _params(sem, vmem=VMEM_LIMIT):
    return pltpu.CompilerParams(dimension_semantics=sem, vmem_limit_bytes=vmem)


def _pick(n, prefs):
    for p in prefs:
        if n % p == 0:
            return p
    return n


def _dot_nt(a, b):
    return lax.dot_general(a, b, (((1,), (1,)), ((), ())), preferred_element_type=F32)


def _ada_kernel(c_ref, w_ref, b_ref, o_ref):
    k = pl.program_id(2)

    @pl.when(k == 0)
    def _():
        o_ref[...] = b_ref[...]

    c = c_ref[...]
    ca = c * (1.0 / (1.0 + jnp.exp(-c)))
    o_ref[...] += jnp.sum(w_ref[...] * ca, axis=0, keepdims=True)


def _ada(c_col, w_ada, b_ada):
    depth, d, n = w_ada.shape
    tk = _pick(d, (2048, 1024, 512, 256, 128))
    tn = _pick(n, (2048, 1024, 512, 256, 128))
    return pl.pallas_call(
        _ada_kernel,
        grid=(depth, n // tn, d // tk),
        in_specs=[
            pl.BlockSpec((tk, 1), lambda l, j, k: (k, 0)),
            pl.BlockSpec((None, tk, tn), lambda l, j, k: (l, k, j)),
            pl.BlockSpec((None, 1, tn), lambda l, j, k: (l, 0, j)),
        ],
        out_specs=pl.BlockSpec((None, 1, tn), lambda l, j, k: (l, 0, j)),
        out_shape=jax.ShapeDtypeStruct((depth, 1, n), F32),
        compiler_params=_params(("parallel", "parallel", "arbitrary")),
        name="ada",
    )(c_col, w_ada, b_ada.reshape(depth, 1, n))


def _norm_mod_kernel(x_ref, g_ref, sc_ref, sh_ref, o_ref):
    x = x_ref[...]
    ms = jnp.mean(x * x, axis=-1, keepdims=True)
    y = x * lax.rsqrt(ms + EPS) * g_ref[...]
    o_ref[...] = (y * (1.0 + sc_ref[...]) + sh_ref[...]).astype(o_ref.dtype)


def _norm_kernel(x_ref, g_ref, o_ref):
    x = x_ref[...]
    ms = jnp.mean(x * x, axis=-1, keepdims=True)
    o_ref[...] = (x * lax.rsqrt(ms + EPS) * g_ref[...]).astype(o_ref.dtype)


def _norm_mod(x, g, sc, sh):
    s, d = x.shape
    tm = _pick(s, (512, 256, 128))
    row = pl.BlockSpec((1, d), lambda i: (0, 0))
    return pl.pallas_call(
        _norm_mod_kernel,
        grid=(s // tm,),
        in_specs=[pl.BlockSpec((tm, d), lambda i: (i, 0)), row, row, row],
        out_specs=pl.BlockSpec((tm, d), lambda i: (i, 0)),
        out_shape=jax.ShapeDtypeStruct((s, d), BF16),
        compiler_params=_params(("parallel",)),
        name="norm_mod",
    )(x, g, sc, sh)


def _final_norm(x, g):
    s, d = x.shape
    tm = _pick(s, (512, 256, 128))
    return pl.pallas_call(
        _norm_kernel,
        grid=(s // tm,),
        in_specs=[pl.BlockSpec((tm, d), lambda i: (i, 0)), pl.BlockSpec((1, d), lambda i: (0, 0))],
        out_specs=pl.BlockSpec((tm, d), lambda i: (i, 0)),
        out_shape=jax.ShapeDtypeStruct((s, d), F32),
        compiler_params=_params(("parallel",)),
        name="final_norm",
    )(x, g)


def _proj_rope_kernel(h_ref, w_ref, tab_ref, o_ref, *, bounds, scales, tn):
    j = pl.program_id(1)
    acc = jnp.dot(h_ref[...], w_ref[...], preferred_element_type=F32)
    b_qa, b_qb, b_qi, b_ka = bounds
    s_qa, s_qb, s_qi = scales
    is_a = (j < b_qa) | ((j >= b_qi) & (j < b_ka))
    scale = jnp.where(j < b_qa, s_qa, jnp.where(j < b_qb, s_qb, jnp.where(j < b_qi, s_qi, 1.0))).astype(F32)

    @pl.when(is_a)
    def _():
        c = tab_ref[:, 0:LANES] * scale
        s = tab_ref[:, LANES:2 * LANES] * scale
        for g in range(tn // LANES):
            xg = acc[:, g * LANES:(g + 1) * LANES]
            o_ref[:, g * LANES:(g + 1) * LANES] = (xg * c + pltpu.roll(xg, 64, 1) * s).astype(o_ref.dtype)

    @pl.when(jnp.logical_not(is_a))
    def _():
        c = tab_ref[:, 2 * LANES:3 * LANES] * scale
        shi = tab_ref[:, 3 * LANES:4 * LANES] * scale
        slo = tab_ref[:, 4 * LANES:5 * LANES] * scale
        for g in range(tn // LANES):
            xg = acc[:, g * LANES:(g + 1) * LANES]
            o_ref[:, g * LANES:(g + 1) * LANES] = (
                xg * c + pltpu.roll(xg, 32, 1) * shi + pltpu.roll(xg, 96, 1) * slo).astype(o_ref.dtype)


def _proj_rope(h, w_r, tabs, seg):
    s, d = h.shape
    n = w_r.shape[1]
    tn = seg["tn"]
    tm = _pick(s, (1024, 512, 256, 128))
    kern = functools.partial(_proj_rope_kernel, bounds=seg["bounds"], scales=seg["scales"], tn=tn)
    return pl.pallas_call(
        kern,
        grid=(s // tm, n // tn),
        in_specs=[
            pl.BlockSpec((tm, d), lambda i, j: (i, 0)),
            pl.BlockSpec((d, tn), lambda i, j: (0, j)),
            pl.BlockSpec((tm, 5 * LANES), lambda i, j: (i, 0)),
        ],
        out_specs=pl.BlockSpec((tm, tn), lambda i, j: (i, j)),
        out_shape=jax.ShapeDtypeStruct((s, n), BF16),
        compiler_params=_params(("parallel", "arbitrary")),
        name="proj_rope",
    )(h, w_r, tabs)


def _proj_v_kernel(h_ref, w_ref, vt_ref, wt_ref, *, n_v, kc):
    acc = jnp.dot(h_ref[...], w_ref[...], preferred_element_type=F32)
    acct = acc.T
    tm = acc.shape[0]
    for c in range(tm // kc):
        vt_ref[c] = acct[:n_v, c * kc:(c + 1) * kc].astype(vt_ref.dtype)
    wt_ref[...] = acct[n_v:n_v + LANES] * (IDX_HEADS ** -0.5)


def _proj_v(h, w_v, n_v, kc):
    s, d = h.shape
    nv = w_v.shape[1]
    tm = _pick(s, (1024, 512))
    tm = max(tm, kc)
    kern = functools.partial(_proj_v_kernel, n_v=n_v, kc=kc)
    return pl.pallas_call(
        kern,
        grid=(s // tm,),
        in_specs=[pl.BlockSpec((tm, d), lambda i: (i, 0)), pl.BlockSpec((d, nv), lambda i: (0, 0))],
        out_specs=[
            pl.BlockSpec((tm // kc, n_v, kc), lambda i: (i, 0, 0)),
            pl.BlockSpec((LANES, tm), lambda i: (0, i)),
        ],
        out_shape=[
            jax.ShapeDtypeStruct((s // kc, n_v, kc), BF16),
            jax.ShapeDtypeStruct((LANES, s), F32),
        ],
        compiler_params=_params(("parallel",)),
        name="proj_v",
    )(h, w_v)


def _dsa_kernel(qa_ref, qi_ref, wt_ref, ka_ref, ke_ref, vt_ref, g_ref, o_ref,
                key_ref, acc_ref, m_ref, l_ref, *, seq, n_sel, hkv, kc):
    i = pl.program_id(0)
    per = kc // TQ
    nchunk = (i + per) // per
    t_idx = i * TQ + lax.broadcasted_iota(I32, (1, TQ), 1)
    row_iota = lax.broadcasted_iota(I32, (kc, TQ), 0)
    n_pairs = IDX_HEADS // 2

    qp = jnp.concatenate([qi_ref[:, p * LANES:(p + 1) * LANES] for p in range(n_pairs)], axis=0)
    wrows = [wt_ref[h:h + 1, :] for h in range(IDX_HEADS)]

    def score_chunk(c, _):
        r0 = pl.multiple_of(c * kc, kc)
        ke = ke_ref[pl.ds(r0, kc), :]
        le = _dot_nt(ke[:, :LANES], qp)
        lo = _dot_nt(ke[:, LANES:], qp)
        sc = jnp.zeros((kc, TQ), F32)
        for p in range(n_pairs):
            sc = sc + wrows[2 * p] * jnp.maximum(le[:, p * TQ:(p + 1) * TQ], 0.0)
            sc = sc + wrows[2 * p + 1] * jnp.maximum(lo[:, p * TQ:(p + 1) * TQ], 0.0)
        sc = jnp.where(r0 + row_iota <= t_idx, sc, NEG)
        b = lax.bitcast_convert_type(sc, I32)
        key_ref[pl.ds(r0, kc), :] = b ^ (lax.shift_right_arithmetic(b, 31) & 0x7FFFFFFF)
        return 0

    lax.fori_loop(0, nchunk, score_chunk, 0)

    n_virtual = (seq - nchunk * kc).astype(F32)

    def count_ge(cand):
        def body(c, acc):
            r0 = pl.multiple_of(c * kc, kc)
            k = key_ref[pl.ds(r0, kc), :]
            return acc + jnp.sum(jnp.where(k >= cand, 1.0, 0.0), axis=0, keepdims=True)
        cnt = lax.fori_loop(0, nchunk, body, jnp.zeros((1, TQ), F32))
        return cnt + jnp.where(cand <= KEY_NEG, n_virtual, 0.0)

    def bit_step(it, u):
        bit = lax.shift_left(jnp.int32(1), jnp.int32(31) - it)
        cand_u = u | bit
        cnt = count_ge(cand_u ^ INT_MIN)
        return jnp.where(cnt >= n_sel, cand_u, u)

    thr = lax.fori_loop(0, 32, bit_step, jnp.zeros((1, TQ), I32)) ^ INT_MIN
    n_tie = n_sel - count_ge(thr + 1)

    m_ref[...] = jnp.full(m_ref.shape, M_INIT, F32)
    l_ref[...] = jnp.zeros(l_ref.shape, F32)
    acc_ref[...] = jnp.zeros(acc_ref.shape, F32)
    qg = [jnp.concatenate([qa_ref[:, (A_GROUP * g + h) * A_HEAD_DIM:(A_GROUP * g + h + 1) * A_HEAD_DIM]
                           for h in range(A_GROUP)], axis=0) for g in range(hkv)]
    ltri = jnp.where(lax.broadcasted_iota(I32, (kc, kc), 0) > lax.broadcasted_iota(I32, (kc, kc), 1),
                     1.0, 0.0).astype(BF16)

    def attend(c, tie_seen):
        r0 = pl.multiple_of(c * kc, kc)
        k = key_ref[pl.ds(r0, kc), :]
        eq = jnp.where(k == thr, 1.0, 0.0)
        rank = jnp.dot(ltri, eq.astype(BF16), preferred_element_type=F32) + tie_seen
        keep = jnp.where(k > thr, 1.0, jnp.where(rank < n_tie, eq, 0.0))
        keep = jnp.where(r0 + row_iota <= t_idx, keep, 0.0)
        bias = jnp.where(keep > 0.5, 0.0, NEG)
        bias4 = jnp.concatenate([bias] * A_GROUP, axis=1)
        kk = ka_ref[pl.ds(r0, kc), :]
        for g in range(hkv):
            st = _dot_nt(kk[:, g * A_HEAD_DIM:(g + 1) * A_HEAD_DIM], qg[g]) + bias4
            m_old = m_ref[g]
            m_new = jnp.maximum(m_old, jnp.max(st, axis=0, keepdims=True))
            alpha = jnp.exp2(m_old - m_new)
            p = jnp.exp2(st - m_new)
            l_ref[g] = l_ref[g] * alpha + jnp.sum(p, axis=0, keepdims=True)
            m_ref[g] = m_new
            vt = vt_ref[c, g * A_HEAD_DIM:(g + 1) * A_HEAD_DIM, :]
            acc_ref[g] = acc_ref[g] * alpha + jnp.dot(vt, p.astype(BF16), preferred_element_type=F32)
        return tie_seen + jnp.sum(eq, axis=0, keepdims=True)

    lax.fori_loop(0, nchunk, attend, jnp.zeros((1, TQ), F32))

    ssq = jnp.zeros((1, TQ), F32)
    for g in range(hkv):
        o = acc_ref[g] / l_ref[g]
        acc_ref[g] = o
        sq = jnp.sum(o * o, axis=0, keepdims=True)
        for h in range(A_GROUP):
            ssq = ssq + sq[:, h * TQ:(h + 1) * TQ]
    rn = lax.rsqrt(ssq / (hkv * A_GROUP * A_HEAD_DIM) + EPS)
    for g in range(hkv):
        for h in range(A_GROUP):
            col = (A_GROUP * g + h) * A_HEAD_DIM
            oh = acc_ref[g, :, h * TQ:(h + 1) * TQ] * rn
            o_ref[:, col:col + A_HEAD_DIM] = (oh.T * g_ref[:, col:col + A_HEAD_DIM]).astype(o_ref.dtype)


def _dsa(pr, vt3, wt, g_a, lay, seq, n_sel):
    d_a = lay["d_a"]
    hkv = lay["hkv_a"]
    nka = hkv * A_HEAD_DIM
    kc = lay["kc"]
    one = pl.Buffered(1)
    kern = functools.partial(_dsa_kernel, seq=seq, n_sel=float(n_sel), hkv=hkv, kc=kc)
    return pl.pallas_call(
        kern,
        grid=(seq // TQ,),
        in_specs=[
            pl.BlockSpec((TQ, d_a), lambda i: (i, 0)),
            pl.BlockSpec((TQ, IDX_HEADS * IDX_DIM), lambda i: (i, lay["off_qi"] // (IDX_HEADS * IDX_DIM))),
            pl.BlockSpec((LANES, TQ), lambda i: (0, i)),
            pl.BlockSpec((seq, nka), lambda i: (0, lay["off_ka"] // nka), pipeline_mode=one),
            pl.BlockSpec((seq, 2 * LANES), lambda i: (0, lay["off_misc"] // (2 * LANES)), pipeline_mode=one),
            pl.BlockSpec((seq // kc, nka, kc), lambda i: (0, 0, 0), pipeline_mode=one),
            pl.BlockSpec((1, d_a), lambda i: (0, 0)),
        ],
        out_specs=pl.BlockSpec((TQ, d_a), lambda i: (i, 0)),
        out_shape=jax.ShapeDtypeStruct((seq, d_a), BF16),
        scratch_shapes=[
            pltpu.VMEM((seq, TQ), I32),
            pltpu.VMEM((hkv, A_HEAD_DIM, A_GROUP * TQ), F32),
            pltpu.VMEM((hkv, 1, A_GROUP * TQ), F32),
            pltpu.VMEM((hkv, 1, A_GROUP * TQ), F32),
        ],
        compiler_params=_params(("arbitrary",)),
        name="dsa",
    )(pr, pr, wt, pr, pr, vt3, g_a)


def _swa_kernel(qb_ref, kp_ref, kc_ref, vp_ref, vc_ref, se_ref, so_ref, g_ref, o_ref, ot_ref,
                *, hkv, kb_off):
    i = pl.program_id(0)
    npair = B_GROUP // 2
    nkb = hkv * B_HEAD_DIM
    kwin = jnp.concatenate([kp_ref[:, kb_off:kb_off + nkb], kc_ref[:, kb_off:kb_off + nkb]],
                           axis=0).astype(F32)
    r = lax.broadcasted_iota(I32, (2 * WINDOW, TQ), 0) - WINDOW
    q = lax.broadcasted_iota(I32, (2 * WINDOW, TQ), 1)
    ok = (r <= q) & (q - r < WINDOW) & (i * WINDOW + r >= 0)
    bias = jnp.where(ok, 0.0, NEG)
    bias4 = jnp.concatenate([bias] * npair, axis=1)
    lane = lax.broadcasted_iota(I32, (2 * WINDOW, LANES), 1)
    ssq = jnp.zeros((1, TQ), F32)
    for g in range(hkv):
        kcol = kwin[:, (g // 2) * LANES:(g // 2 + 1) * LANES]
        rolled = pltpu.roll(kcol, 64, 1)
        lo_src, hi_src = (kcol, rolled) if g % 2 == 0 else (rolled, kcol)
        k_even = jnp.where(lane < 64, lo_src, 0.0).astype(BF16)
        k_odd = jnp.where(lane >= 64, hi_src, 0.0).astype(BF16)
        qs = jnp.concatenate([qb_ref[:, (npair * g + p) * LANES:(npair * g + p + 1) * LANES]
                              for p in range(npair)], axis=0)
        vp = vp_ref[g * B_HEAD_DIM:(g + 1) * B_HEAD_DIM, :]
        vc = vc_ref[g * B_HEAD_DIM:(g + 1) * B_HEAD_DIM, :]
        outs = []
        for kmat, s_ref in ((k_even, se_ref), (k_odd, so_ref)):
            st = _dot_nt(kmat, qs) + bias4
            sink = s_ref[g]
            m = jnp.maximum(jnp.max(st, axis=0, keepdims=True), sink)
            p = jnp.exp2(st - m)
            den = jnp.sum(p, axis=0, keepdims=True) + jnp.exp2(sink - m)
            pb = p.astype(BF16)
            ot = (jnp.dot(vp, pb[:WINDOW], preferred_element_type=F32)
                  + jnp.dot(vc, pb[WINDOW:], preferred_element_type=F32))
            outs.append(ot / den)
        for p in range(npair):
            blk = jnp.concatenate([outs[0][:, p * TQ:(p + 1) * TQ], outs[1][:, p * TQ:(p + 1) * TQ]],
                                  axis=0)
            ot_ref[npair * g + p] = blk
            ssq = ssq + jnp.sum(blk * blk, axis=0, keepdims=True)
    rn = lax.rsqrt(ssq / (hkv * B_GROUP * B_HEAD_DIM) + EPS)
    for c in range(hkv * npair):
        o_ref[:, c * LANES:(c + 1) * LANES] = (
            (ot_ref[c] * rn).T * g_ref[:, c * LANES:(c + 1) * LANES]).astype(o_ref.dtype)


def _swa(pr, vt3, sink_e, sink_o, g_b, lay, seq):
    d_b = lay["d_b"]
    hkv = lay["hkv_b"]
    kc = lay["kc"]
    per = kc // TQ
    nvb = hkv * B_HEAD_DIM
    row_blk = lay["n_va"] // nvb
    off_kb = lay["off_misc"] + 2 * LANES
    assert nvb % LANES == 0 and off_kb % nvb == 0
    col_kb = off_kb // nvb
    prev = lambda i: jnp.maximum(i - 1, 0)
    kern = functools.partial(_swa_kernel, hkv=hkv, kb_off=0)
    return pl.pallas_call(
        kern,
        grid=(seq // TQ,),
        in_specs=[
            pl.BlockSpec((TQ, d_b), lambda i: (i, lay["off_qb"] // d_b)),
            pl.BlockSpec((TQ, nvb), lambda i: (prev(i), col_kb)),
            pl.BlockSpec((TQ, nvb), lambda i: (i, col_kb)),
            pl.BlockSpec((None, nvb, TQ), lambda i: (prev(i) // per, row_blk, prev(i) % per)),
            pl.BlockSpec((None, nvb, TQ), lambda i: (i // per, row_blk, i % per)),
            pl.BlockSpec((hkv, 1, (B_GROUP // 2) * TQ), lambda i: (0, 0, 0)),
            pl.BlockSpec((hkv, 1, (B_GROUP // 2) * TQ), lambda i: (0, 0, 0)),
            pl.BlockSpec((1, d_b), lambda i: (0, 0)),
        ],
        out_specs=pl.BlockSpec((TQ, d_b), lambda i: (i, 0)),
        out_shape=jax.ShapeDtypeStruct((seq, d_b), BF16),
        scratch_shapes=[pltpu.VMEM((hkv * B_GROUP // 2, LANES, TQ), F32)],
        compiler_params=_params(("parallel",)),
        name="swa",
    )(pr, pr, pr, vt3, vt3, sink_e, sink_o, g_b)


def _oproj_kernel(na_ref, nb_ref, wa_ref, wb_ref, x_ref, gt_ref, o_ref):
    acc = jnp.dot(na_ref[...], wa_ref[...], preferred_element_type=F32)
    acc = acc + jnp.dot(nb_ref[...], wb_ref[...], preferred_element_type=F32)
    o_ref[...] = x_ref[...] + gt_ref[...] * acc


def _oproj(na, nb, w_out, x, gate):
    s, d_a = na.shape
    d_b = nb.shape[1]
    d = x.shape[1]
    assert d_a == d_b
    tm = _pick(s, (1024, 512, 256, 128))
    tn = _pick(d, (512, 256, 128))
    return pl.pallas_call(
        _oproj_kernel,
        grid=(s // tm, d // tn),
        in_specs=[
            pl.BlockSpec((tm, d_a), lambda i, j: (i, 0)),
            pl.BlockSpec((tm, d_b), lambda i, j: (i, 0)),
            pl.BlockSpec((d_a, tn), lambda i, j: (0, j)),
            pl.BlockSpec((d_b, tn), lambda i, j: (1, j)),
            pl.BlockSpec((tm, tn), lambda i, j: (i, j)),
            pl.BlockSpec((1, tn), lambda i, j: (0, j)),
        ],
        out_specs=pl.BlockSpec((tm, tn), lambda i, j: (i, j)),
        out_shape=jax.ShapeDtypeStruct((s, d), F32),
        compiler_params=_params(("parallel", "arbitrary")),
        name="oproj",
    )(na, nb, w_out, w_out, x, gate)


def _up_kernel(h_ref, wg_ref, wv_ref, cwg_ref, cwv_ref, cbg_ref, cbv_ref, o_ref, eg_ref, ev_ref, *, tm):
    i = pl.program_id(1)
    halo = 8

    @pl.when(i == 0)
    def _():
        eg_ref[0:halo, :] = jnp.zeros((halo, eg_ref.shape[1]), F32)
        ev_ref[0:halo, :] = jnp.zeros((halo, ev_ref.shape[1]), F32)

    @pl.when(i > 0)
    def _():
        eg_ref[0:halo, :] = eg_ref[tm:tm + halo, :]
        ev_ref[0:halo, :] = ev_ref[tm:tm + halo, :]

    h = h_ref[...]
    eg_ref[halo:halo + tm, :] = jnp.dot(h, wg_ref[...], preferred_element_type=F32)
    ev_ref[halo:halo + tm, :] = jnp.dot(h, wv_ref[...], preferred_element_type=F32)

    def conv(e_ref, cw_ref, cb_ref):
        y = cb_ref[...] + cw_ref[CONV_WIDTH - 1:CONV_WIDTH, :] * e_ref[halo:halo + tm, :]
        for k in range(1, CONV_WIDTH):
            y = y + cw_ref[CONV_WIDTH - 1 - k:CONV_WIDTH - k, :] * e_ref[halo - k:halo - k + tm, :]
        return y

    gte = conv(eg_ref, cwg_ref, cbg_ref)
    val = conv(ev_ref, cwv_ref, cbv_ref)
    o_ref[...] = (gte * (1.0 / (1.0 + jnp.exp(-gte))) * val).astype(o_ref.dtype)


def _up(h, wg, wv, cwg, cwv, cbg, cbv):
    s, d = h.shape
    fp = wg.shape[1]
    tm = _pick(s, (1024, 512, 256, 128))
    tn = _pick(fp, (512, 256, 128))
    kern = functools.partial(_up_kernel, tm=tm)
    wspec = pl.BlockSpec((d, tn), lambda j, i: (0, j))
    cspec = pl.BlockSpec((CONV_WIDTH, tn), lambda j, i: (0, j))
    bspec = pl.BlockSpec((1, tn), lambda j, i: (0, j))
    return pl.pallas_call(
        kern,
        grid=(fp // tn, s // tm),
        in_specs=[pl.BlockSpec((tm, d), lambda j, i: (i, 0)), wspec, wspec, cspec, cspec, bspec, bspec],
        out_specs=pl.BlockSpec((tm, tn), lambda j, i: (i, j)),
        out_shape=jax.ShapeDtypeStruct((s, fp), BF16),
        scratch_shapes=[pltpu.VMEM((tm + 8, tn), F32), pltpu.VMEM((tm + 8, tn), F32)],
        compiler_params=_params(("arbitrary", "arbitrary")),
        name="up_conv_gate",
    )(h, wg, wv, cwg, cwv, cbg, cbv)


def _down_kernel(a_ref, w_ref, x_ref, gt_ref, o_ref, acc_ref):
    k = pl.program_id(2)

    @pl.when(k == 0)
    def _():
        acc_ref[...] = jnp.zeros(acc_ref.shape, F32)

    acc_ref[...] += jnp.dot(a_ref[...], w_ref[...], preferred_element_type=F32)

    @pl.when(k == pl.num_programs(2) - 1)
    def _():
        o_ref[...] = x_ref[...] + gt_ref[...] * acc_ref[...]


def _down(act, wd, x, gate):
    s, fp = act.shape
    d = x.shape[1]
    tm = _pick(s, (1024, 512, 256, 128))
    tn = _pick(d, (1024, 512, 256, 128))
    tk = fp
    for cand in (2816, 2048, 1536, 1408, 1024, 512, 256, 128):
        if fp % cand == 0:
            tk = cand
            break
    return pl.pallas_call(
        _down_kernel,
        grid=(s // tm, d // tn, fp // tk),
        in_specs=[
            pl.BlockSpec((tm, tk), lambda i, j, k: (i, k)),
            pl.BlockSpec((tk, tn), lambda i, j, k: (k, j)),
            pl.BlockSpec((tm, tn), lambda i, j, k: (i, j)),
            pl.BlockSpec((1, tn), lambda i, j, k: (0, j)),
        ],
        out_specs=pl.BlockSpec((tm, tn), lambda i, j, k: (i, j)),
        out_shape=jax.ShapeDtypeStruct((s, d), F32),
        scratch_shapes=[pltpu.VMEM((tm, tn), F32)],
        compiler_params=_params(("parallel", "parallel", "arbitrary")),
        name="down",
    )(act, wd, x, gate)


def _layout(d_model, seq):
    d_a = d_model // 2
    d_b = d_model - d_a
    ha = d_a // A_HEAD_DIM
    hkv_a = ha // A_GROUP
    hb = d_b // B_HEAD_DIM
    hkv_b = hb // B_GROUP
    n_qi = IDX_HEADS * IDX_DIM
    n_ka = hkv_a * A_HEAD_DIM
    n_kb = hkv_b * B_HEAD_DIM
    misc_raw = 2 * LANES + n_kb
    seg_w = [d_a, d_b, n_qi, n_ka]
    tn = 512
    while any(w % tn for w in seg_w) or tn > misc_raw + LANES:
        tn //= 2
    misc_w = -(-misc_raw // tn) * tn
    off_qb = d_a
    off_qi = off_qb + d_b
    off_ka = off_qi + n_qi
    off_misc = off_ka + n_ka
    n_r = off_misc + misc_w
    bounds = (off_qb // tn, off_qi // tn, off_ka // tn, off_misc // tn)
    scales = (A_HEAD_DIM ** -0.5 * LOG2E, B_HEAD_DIM ** -0.5 * LOG2E, IDX_DIM ** -0.5)
    kc = min(KC, seq)
    assert d_a == d_b and seq % kc == 0 and kc % TQ == 0 and off_qi % n_qi == 0 and off_ka % n_ka == 0
    assert off_misc % (2 * LANES) == 0
    n_va = n_ka
    n_vb = n_kb
    assert n_va % n_vb == 0
    return dict(d_a=d_a, d_b=d_b, ha=ha, hkv_a=hkv_a, hb=hb, hkv_b=hkv_b, n_qi=n_qi, n_ka=n_ka, n_kb=n_kb,
                tn=tn, misc_w=misc_w, off_qb=off_qb, off_qi=off_qi, off_ka=off_ka, off_misc=off_misc,
                n_r=n_r, bounds=bounds, scales=scales, kc=kc, n_va=n_va, n_vb=n_vb)


def _prep_w_in(w_in, lay):
    d_a, d_b = lay["d_a"], lay["d_b"]
    widths = (d_a, lay["n_ka"], lay["n_ka"], lay["n_qi"], IDX_DIM, IDX_HEADS, d_b, lay["n_kb"], lay["n_kb"])
    offs = np.concatenate([[0], np.cumsum(widths)])
    qa, ka, va, qi, ki, wi, qb, kb, vb = [w_in[..., offs[n]:offs[n + 1]] for n in range(9)]
    zk = jnp.zeros_like(ki)
    pad_m = lay["misc_w"] - (2 * LANES + lay["n_kb"])
    parts = [qa, qb, qi, ka, ki, zk, zk, ki, kb]
    if pad_m:
        parts.append(jnp.zeros(w_in.shape[:-1] + (pad_m,), w_in.dtype))
    w_r = jnp.concatenate(parts, axis=-1).astype(BF16)
    padw = jnp.zeros(w_in.shape[:-1] + (LANES - IDX_HEADS,), w_in.dtype)
    w_v = jnp.concatenate([va, vb, wi, padw], axis=-1).astype(BF16)
    return w_r, w_v


def _rope_tabs(positions):
    pos = positions[0].astype(F32)
    inv_a = ROPE_THETA ** (-jnp.arange(0, A_HEAD_DIM, 2, dtype=F32) / A_HEAD_DIM)
    ang_a = pos[:, None] * inv_a
    ca, sa = jnp.cos(ang_a), jnp.sin(ang_a)
    inv_6 = ROPE_THETA ** (-jnp.arange(0, B_HEAD_DIM, 2, dtype=F32) / B_HEAD_DIM)
    ang_6 = pos[:, None] * inv_6
    c6, s6 = jnp.cos(ang_6), jnp.sin(ang_6)
    z6 = jnp.zeros_like(s6)
    return jnp.concatenate([
        ca, ca, -sa, sa,
        c6, c6, c6, c6,
        z6, s6, z6, s6,
        -s6, z6, -s6, z6], axis=-1)


def kernel(x, c, positions, w_ada, b_ada, g_mix, w_in, g_out_a, g_out_b, sinks, w_out, g_ffn, w_up,
           conv_w, conv_b, w_down, g_final):
    b, seq, d = x.shape
    assert b == 1 and IDX_DIM == B_HEAD_DIM
    depth = w_ada.shape[0]
    f = w_down.shape[1]
    lay = _layout(d, seq)
    n_sel = min(TOPK_MAX, seq // 4)
    hkv_b = lay["hkv_b"]

    w_r, w_v = _prep_w_in(w_in, lay)
    w_out_b = w_out.astype(BF16)
    fp = -(-f // 512) * 512 if f >= 512 else f
    padf = fp - f
    pad_cols = lambda a: jnp.pad(a, [(0, 0)] * (a.ndim - 1) + [(0, padf)])
    w_g = pad_cols(w_up[..., :f]).astype(BF16)
    w_val = pad_cols(w_up[..., f:]).astype(BF16)
    cw_g, cw_v = pad_cols(conv_w[..., :f]), pad_cols(conv_w[..., f:])
    cb_g, cb_v = pad_cols(conv_b[..., :f])[:, None, :], pad_cols(conv_b[..., f:])[:, None, :]
    w_d = jnp.pad(w_down, ((0, 0), (0, padf), (0, 0))).astype(BF16)
    tabs = _rope_tabs(positions)
    sk = (sinks * LOG2E).reshape(depth, hkv_b, B_GROUP // 2, 2)
    rep = lambda a: jnp.repeat(a, TQ, axis=-1).reshape(depth, hkv_b, 1, (B_GROUP // 2) * TQ)
    sink_e, sink_o = rep(sk[..., 0]), rep(sk[..., 1])

    mod = _ada(c.reshape(d, 1), w_ada, b_ada)
    xs = x[0]
    for l in range(depth):
        sh_m, sc_m, gt_m, sh_f, sc_f, gt_f = [mod[l, :, n * d:(n + 1) * d] for n in range(N_MOD)]
        h = _norm_mod(xs, g_mix[l][None, :], sc_m, sh_m)
        pr = _proj_rope(h, w_r[l], tabs, lay)
        vt3, wt = _proj_v(h, w_v[l], lay["n_va"] + lay["n_vb"], lay["kc"])
        na = _dsa(pr, vt3, wt, g_out_a[l][None, :], lay, seq, n_sel)
        nb = _swa(pr, vt3, sink_e[l], sink_o[l], g_out_b[l][None, :], lay, seq)
        xs = _oproj(na, nb, w_out_b[l], xs, gt_m)
        h = _norm_mod(xs, g_ffn[l][None, :], sc_f, sh_f)
        act = _up(h, w_g[l], w_val[l], cw_g[l], cw_v[l], cb_g[l], cb_v[l])
        xs = _down(act, w_d[l], xs, gt_f)
    return _final_norm(xs, g_final[None, :])[None]
```

```python
import functools

import numpy as np
import jax
import jax.numpy as jnp
from jax import lax
from jax.experimental import pallas as pl
from jax.experimental.pallas import tpu as pltpu

A_HEAD_DIM = 128
A_GROUP = 4
IDX_HEADS = 16
IDX_DIM = 64
TOPK_MAX = 256
B_HEAD_DIM = 64
B_GROUP = 8
WINDOW = 128
CONV_WIDTH = 3
ROPE_THETA = 10000.0
EPS = 1e-6
NEG = -1e30
M_INIT = -1e29
N_MOD = 6
LOG2E = 1.4426950408889634

LANES = 128
VMEM_LIMIT = 56 * 1024 * 1024

TQ = 128
KC = 512

F32 = jnp.float32
BF16 = jnp.bfloat16
I32 = jnp.int32
I16 = jnp.int16

_NEG_BITS = int(np.float32(NEG).view(np.int32))
KEY_NEG = _NEG_BITS ^ ((_NEG_BITS >> 31) & 0x7FFFFFFF)
NEG_HI = KEY_NEG >> 16
NEG_LO = (KEY_NEG & 0xFFFF) - 32768
I16_MIN = -32768


def _params(sem, vmem=VMEM_LIMIT):
    return pltpu.CompilerParams(dimension_semantics=sem, vmem_limit_bytes=vmem)


def _pick(n, prefs):
    for p in prefs:
        if n % p == 0:
            return p
    return n


def _dot_nt(a, b):
    return lax.dot_general(a, b, (((1,), (1,)), ((), ())), preferred_element_type=F32)


def _tree(op, parts):
    parts = list(parts)
    while len(parts) > 1:
        nxt = [op(parts[a], parts[a + 1]) for a in range(0, len(parts) - 1, 2)]
        if len(parts) % 2:
            nxt.append(parts[-1])
        parts = nxt
    return parts[0]


def _col_reduce(op, red, x):
    slabs = [x[r:r + 8] for r in range(0, x.shape[0], 8)]
    return red(_tree(op, slabs), axis=0, keepdims=True)


def _ada_kernel(c_ref, w_ref, b_ref, o_ref):
    k = pl.program_id(2)

    @pl.when(k == 0)
    def _():
        o_ref[...] = b_ref[...]

    c = c_ref[...]
    ca = c * (1.0 / (1.0 + jnp.exp(-c)))
    o_ref[...] += jnp.sum(w_ref[...] * ca, axis=0, keepdims=True)


def _ada(c_col, w_ada, b_ada):
    depth, d, n = w_ada.shape
    tk = _pick(d, (2048, 1024, 512, 256, 128))
    tn = _pick(n, (2048, 1024, 512, 256, 128))
    return pl.pallas_call(
        _ada_kernel,
        grid=(depth, n // tn, d // tk),
        in_specs=[
            pl.BlockSpec((tk, 1), lambda l, j, k: (k, 0)),
            pl.BlockSpec((None, tk, tn), lambda l, j, k: (l, k, j)),
            pl.BlockSpec((None, 1, tn), lambda l, j, k: (l, 0, j)),
        ],
        out_specs=pl.BlockSpec((None, 1, tn), lambda l, j, k: (l, 0, j)),
        out_shape=jax.ShapeDtypeStruct((depth, 1, n), F32),
        compiler_params=_params(("parallel", "parallel", "arbitrary")),
        name="ada",
    )(c_col, w_ada, b_ada.reshape(depth, 1, n))


def _norm_mod_kernel(x_ref, g_ref, sc_ref, sh_ref, o_ref):
    x = x_ref[...]
    ms = jnp.mean(x * x, axis=-1, keepdims=True)
    y = x * lax.rsqrt(ms + EPS) * g_ref[...]
    o_ref[...] = (y * (1.0 + sc_ref[...]) + sh_ref[...]).astype(o_ref.dtype)


def _norm_kernel(x_ref, g_ref, o_ref):
    x = x_ref[...]
    ms = jnp.mean(x * x, axis=-1, keepdims=True)
    o_ref[...] = (x * lax.rsqrt(ms + EPS) * g_ref[...]).astype(o_ref.dtype)


def _norm_mod(x, g, sc, sh):
    s, d = x.shape
    tm = _pick(s, (512, 256, 128))
    row = pl.BlockSpec((1, d), lambda i: (0, 0))
    return pl.pallas_call(
        _norm_mod_kernel,
        grid=(s // tm,),
        in_specs=[pl.BlockSpec((tm, d), lambda i: (i, 0)), row, row, row],
        out_specs=pl.BlockSpec((tm, d), lambda i: (i, 0)),
        out_shape=jax.ShapeDtypeStruct((s, d), BF16),
        compiler_params=_params(("parallel",)),
        name="norm_mod",
    )(x, g, sc, sh)


def _final_norm(x, g):
    s, d = x.shape
    tm = _pick(s, (512, 256, 128))
    return pl.pallas_call(
        _norm_kernel,
        grid=(s // tm,),
        in_specs=[pl.BlockSpec((tm, d), lambda i: (i, 0)), pl.BlockSpec((1, d), lambda i: (0, 0))],
        out_specs=pl.BlockSpec((tm, d), lambda i: (i, 0)),
        out_shape=jax.ShapeDtypeStruct((s, d), F32),
        compiler_params=_params(("parallel",)),
        name="final_norm",
    )(x, g)


def _proj_rope_kernel(h_ref, w_ref, tab_ref, o_ref, *, bounds, scales, tn, parts):
    j = pl.program_id(1)
    b_qa, b_qb, b_qi, b_ka = bounds
    s_qa, s_qb, s_qi = scales
    is_a = (j < b_qa) | ((j >= b_qi) & (j < b_ka))
    scale = jnp.where(j < b_qa, s_qa, jnp.where(j < b_qb, s_qb, jnp.where(j < b_qi, s_qi, 1.0))).astype(F32)
    n = h_ref.shape[0] // parts
    w = w_ref[...]

    def rope128(acc, r):
        c = tab_ref[r, 0:LANES] * scale
        s = tab_ref[r, LANES:2 * LANES] * scale
        for g in range(tn // LANES):
            xg = acc[:, g * LANES:(g + 1) * LANES]
            o_ref[r, g * LANES:(g + 1) * LANES] = (xg * c + pltpu.roll(xg, 64, 1) * s).astype(o_ref.dtype)

    def rope64(acc, r):
        c = tab_ref[r, 2 * LANES:3 * LANES] * scale
        shi = tab_ref[r, 3 * LANES:4 * LANES] * scale
        slo = tab_ref[r, 4 * LANES:5 * LANES] * scale
        for g in range(tn // LANES):
            xg = acc[:, g * LANES:(g + 1) * LANES]
            o_ref[r, g * LANES:(g + 1) * LANES] = (
                xg * c + pltpu.roll(xg, 32, 1) * shi + pltpu.roll(xg, 96, 1) * slo).astype(o_ref.dtype)

    def body(epilogue):
        for p in range(parts):
            r = slice(p * n, (p + 1) * n)
            epilogue(jnp.dot(h_ref[r, :], w, preferred_element_type=F32), r)

    pl.when(is_a)(lambda: body(rope128))
    pl.when(jnp.logical_not(is_a))(lambda: body(rope64))


def _proj_rope(h, w_r, tabs, seg):
    s, d = h.shape
    n = w_r.shape[1]
    tn = seg["tn"]
    tm = _pick(s, (1024, 512, 256, 128))
    kern = functools.partial(_proj_rope_kernel, bounds=seg["bounds"], scales=seg["scales"], tn=tn,
                             parts=4 if tm % 512 == 0 else 1)
    return pl.pallas_call(
        kern,
        grid=(s // tm, n // tn),
        in_specs=[
            pl.BlockSpec((tm, d), lambda i, j: (i, 0)),
            pl.BlockSpec((d, tn), lambda i, j: (0, j)),
            pl.BlockSpec((tm, 5 * LANES), lambda i, j: (i, 0)),
        ],
        out_specs=pl.BlockSpec((tm, tn), lambda i, j: (i, j)),
        out_shape=jax.ShapeDtypeStruct((s, n), BF16),
        compiler_params=_params(("parallel", "arbitrary")),
        name="proj_rope",
    )(h, w_r, tabs)


VA_PAD = 16
VA_ROWS = A_HEAD_DIM + VA_PAD


def _proj_v_kernel(h_ref, w_ref, va_ref, vb_ref, wt_ref, *, n_va, n_vb, kc):
    acc = jnp.dot(h_ref[...], w_ref[...], preferred_element_type=F32)
    acct = acc.T
    tm = acc.shape[0]
    ones_rows = jnp.where(lax.broadcasted_iota(I32, (VA_PAD, kc), 0) == 0, 1.0, 0.0).astype(va_ref.dtype)
    for c in range(tm // kc):
        cols = slice(c * kc, (c + 1) * kc)
        for g in range(n_va // A_HEAD_DIM):
            va_ref[c, g * VA_ROWS:g * VA_ROWS + A_HEAD_DIM, :] = (
                acct[g * A_HEAD_DIM:(g + 1) * A_HEAD_DIM, cols].astype(va_ref.dtype))
            va_ref[c, g * VA_ROWS + A_HEAD_DIM:(g + 1) * VA_ROWS, :] = ones_rows
        vb_ref[c] = acct[n_va:n_va + n_vb, cols].astype(vb_ref.dtype)
    wt_ref[...] = acct[n_va + n_vb:n_va + n_vb + LANES] * (IDX_HEADS ** -0.5)


def _proj_v(h, w_v, n_va, n_vb, kc):
    s, d = h.shape
    nv = w_v.shape[1]
    tm = _pick(s, (1024, 512))
    tm = max(tm, kc)
    n_va_aug = n_va // A_HEAD_DIM * VA_ROWS
    kern = functools.partial(_proj_v_kernel, n_va=n_va, n_vb=n_vb, kc=kc)
    return pl.pallas_call(
        kern,
        grid=(s // tm,),
        in_specs=[pl.BlockSpec((tm, d), lambda i: (i, 0)), pl.BlockSpec((d, nv), lambda i: (0, 0))],
        out_specs=[
            pl.BlockSpec((tm // kc, n_va_aug, kc), lambda i: (i, 0, 0)),
            pl.BlockSpec((tm // kc, n_vb, kc), lambda i: (i, 0, 0)),
            pl.BlockSpec((LANES, tm), lambda i: (0, i)),
        ],
        out_shape=[
            jax.ShapeDtypeStruct((s // kc, n_va_aug, kc), BF16),
            jax.ShapeDtypeStruct((s // kc, n_vb, kc), BF16),
            jax.ShapeDtypeStruct((LANES, s), F32),
        ],
        compiler_params=_params(("parallel",)),
        name="proj_v",
    )(h, w_v)


def _dsa_kernel(qa_ref, qi_ref, wt_ref, ka_ref, ke_ref, vt_ref, g_ref, o_ref,
                key_ref, hi_ref, lo_ref, acc_ref, m_ref, *, seq, n_sel, hkv, kc):
    i = pl.program_id(0)
    per = kc // TQ
    nchunk = (i + per) // per
    t_idx = i * TQ + lax.broadcasted_iota(I32, (1, TQ), 1)
    row_iota = lax.broadcasted_iota(I32, (kc, TQ), 0)
    n_pairs = IDX_HEADS // 2
    pairs_per_dot = 2
    n_dots = n_pairs // pairs_per_dot

    def rows(c):
        return pl.ds(c * kc if isinstance(c, int) else pl.multiple_of(c * kc, kc), kc)

    qp = [jnp.concatenate([qi_ref[:, (b * pairs_per_dot + p) * LANES:(b * pairs_per_dot + p + 1) * LANES]
                           for p in range(pairs_per_dot)], axis=0) for b in range(n_dots)]
    wrows = [wt_ref[h:h + 1, :] for h in range(IDX_HEADS)]

    def score_chunk(c, _):
        ke = ke_ref[rows(c), :]
        k_even, k_odd = ke[:, :LANES], ke[:, LANES:]
        logits = lambda b: (_dot_nt(k_even, qp[b]), _dot_nt(k_odd, qp[b]))
        cur = logits(0)
        sc = jnp.zeros((kc, TQ), F32)
        for b in range(n_dots):
            nxt = logits(b + 1) if b + 1 < n_dots else None
            for p in range(pairs_per_dot):
                h = 2 * (b * pairs_per_dot + p)
                sc = sc + wrows[h] * jnp.maximum(cur[0][:, p * TQ:(p + 1) * TQ], 0.0)
                sc = sc + wrows[h + 1] * jnp.maximum(cur[1][:, p * TQ:(p + 1) * TQ], 0.0)
            cur = nxt
        sc = jnp.where(c * kc + row_iota <= t_idx, sc, NEG)
        b32 = lax.bitcast_convert_type(sc, I32)
        key = b32 ^ (lax.shift_right_arithmetic(b32, 31) & 0x7FFFFFFF)
        key_ref[rows(c), :] = key
        hi_ref[rows(c), :] = lax.shift_right_arithmetic(key, 16).astype(I16)
        lo_ref[rows(c), :] = ((key & 0xFFFF) - 32768).astype(I16)
        return 0

    lax.fori_loop(0, nchunk, score_chunk, 0)

    n_virtual = (seq - nchunk * kc).astype(F32)
    cslab = 64

    def count16(ref, cand, strict):
        def body(c, acc):
            v = ref[rows(c), :]
            hit = jnp.where((v > cand) if strict else (v >= cand), jnp.int16(1), jnp.int16(0))
            return acc + _tree(jnp.add, [hit[r:r + cslab] for r in range(0, kc, cslab)])
        part = lax.fori_loop(0, nchunk, body, jnp.zeros((cslab, TQ), I16))
        return _col_reduce(jnp.add, jnp.sum, part.astype(F32))

    def bisect16(count_fn, cnt_all):
        def step(it, carry):
            u, cnt_u = carry
            cand_u = u | lax.shift_left(jnp.int32(1), jnp.int32(15) - it)
            cnt = count_fn(cand_u - 32768)
            ok = cnt >= n_sel
            return jnp.where(ok, cand_u, u), jnp.where(ok, cnt, cnt_u)
        return lax.fori_loop(0, 16, step, (jnp.zeros((1, TQ), I32), cnt_all))

    def count_hi_ge(cand):
        return count16(hi_ref, cand.astype(I16), False) + jnp.where(cand <= NEG_HI, n_virtual, 0.0)

    u_hi, cnt_hi_ge = bisect16(count_hi_ge, jnp.full((1, TQ), float(seq), F32))
    t_hi = u_hi - 32768
    t_hi16 = t_hi.astype(I16)
    cnt_hi_gt = count16(hi_ref, t_hi16, True) + jnp.where(t_hi < NEG_HI, n_virtual, 0.0)

    def mask_lo(c, _):
        lo_ref[rows(c), :] = jnp.where(hi_ref[rows(c), :] == t_hi16, lo_ref[rows(c), :], jnp.int16(I16_MIN))
        return 0

    lax.fori_loop(0, nchunk, mask_lo, 0)
    virt_lo = jnp.where(t_hi == NEG_HI, n_virtual, 0.0)

    def count_lo_ge(cand):
        return (cnt_hi_gt + count16(lo_ref, cand.astype(I16), False)
                + jnp.where(cand <= NEG_LO, virt_lo, 0.0))

    u_lo, cnt_thr = bisect16(count_lo_ge, cnt_hi_ge)
    t_lo = u_lo - 32768
    cnt_gt = cnt_hi_gt + count16(lo_ref, t_lo.astype(I16), True) + jnp.where(t_lo < NEG_LO, virt_lo, 0.0)
    thr = lax.shift_left(t_hi, 16) | u_lo
    n_tie = n_sel - cnt_gt
    excess_ties = jnp.max(cnt_thr) > n_sel

    m_ref[...] = jnp.full(m_ref.shape, M_INIT, F32)
    acc_ref[...] = jnp.zeros(acc_ref.shape, F32)
    eye = jnp.where(lax.broadcasted_iota(I32, (TQ, LANES), 0) == lax.broadcasted_iota(I32, (TQ, LANES), 1),
                    1.0, 0.0).astype(BF16)
    rhs = [jnp.concatenate(
        [jnp.concatenate([qa_ref[:, (A_GROUP * g + h) * A_HEAD_DIM:(A_GROUP * g + h + 1) * A_HEAD_DIM], eye],
                         axis=1) for h in range(A_GROUP)], axis=0) for g in range(hkv)]

    def qk(c, bias_b, g):
        kg = ka_ref[rows(c), g * A_HEAD_DIM:(g + 1) * A_HEAD_DIM]
        return _dot_nt(jnp.concatenate([kg, bias_b], axis=1), rhs[g])

    def attend_chunk(c, bias_b, st, next_first):
        for g in range(hkv):
            if g + 1 < hkv:
                st_next = qk(c, bias_b, g + 1)
            else:
                st_next = next_first() if next_first is not None else None
            m_old = m_ref[g]
            m_new = jnp.maximum(m_old, _col_reduce(jnp.maximum, jnp.max, st))
            alpha = jnp.exp2(m_old - m_new)
            p = jnp.exp2(st - m_new).astype(BF16)
            m_ref[g] = m_new
            vt = vt_ref[c, g * VA_ROWS:(g + 1) * VA_ROWS, :]
            acc_ref[g] = acc_ref[g] * alpha + jnp.dot(vt, p, preferred_element_type=F32)
            st = st_next
        return st

    def plain_bias(c):
        return jnp.where(key_ref[rows(c), :] >= thr, 0.0, NEG).astype(BF16)

    def attend_plain(c, st):
        c_next = jnp.minimum(c + 1, nchunk - 1)
        return attend_chunk(c, plain_bias(c), st, lambda: qk(c_next, plain_bias(c_next), 0))

    def attend_ties(c, tie_seen):
        k = key_ref[rows(c), :]
        ltri = jnp.where(lax.broadcasted_iota(I32, (kc, kc), 0) > lax.broadcasted_iota(I32, (kc, kc), 1),
                         1.0, 0.0).astype(BF16)
        eq = jnp.where(k == thr, 1.0, 0.0)
        rank = jnp.dot(ltri, eq.astype(BF16), preferred_element_type=F32) + tie_seen
        keep = jnp.where(k > thr, 1.0, jnp.where(rank < n_tie, eq, 0.0))
        keep = jnp.where(c * kc + row_iota <= t_idx, keep, 0.0)
        bias_b = jnp.where(keep > 0.5, 0.0, NEG).astype(BF16)
        attend_chunk(c, bias_b, qk(c, bias_b, 0), None)
        return tie_seen + _col_reduce(jnp.add, jnp.sum, eq)

    def run_ties():
        lax.fori_loop(0, nchunk, attend_ties, jnp.zeros((1, TQ), F32))

    def run_plain():
        lax.fori_loop(0, nchunk, attend_plain, qk(0, plain_bias(0), 0))

    lax.cond(excess_ties, run_ties, run_plain)

    ssq = jnp.zeros((1, TQ), F32)
    for g in range(hkv):
        o = acc_ref[g, 0:A_HEAD_DIM, :] / acc_ref[g, A_HEAD_DIM:A_HEAD_DIM + 1, :]
        acc_ref[g, 0:A_HEAD_DIM, :] = o
        sq = jnp.sum(o * o, axis=0, keepdims=True)
        for h in range(A_GROUP):
            ssq = ssq + sq[:, h * TQ:(h + 1) * TQ]
    rn = lax.rsqrt(ssq / (hkv * A_GROUP * A_HEAD_DIM) + EPS)
    for g in range(hkv):
        for h in range(A_GROUP):
            col = (A_GROUP * g + h) * A_HEAD_DIM
            oh = acc_ref[g, 0:A_HEAD_DIM, h * TQ:(h + 1) * TQ] * rn
            o_ref[:, col:col + A_HEAD_DIM] = (oh.T * g_ref[:, col:col + A_HEAD_DIM]).astype(o_ref.dtype)


def _dsa(pr, vat3, wt, g_a, lay, seq, n_sel):
    d_a = lay["d_a"]
    hkv = lay["hkv_a"]
    nka = hkv * A_HEAD_DIM
    kc = lay["kc"]
    assert TQ == LANES
    one = pl.Buffered(1)
    kern = functools.partial(_dsa_kernel, seq=seq, n_sel=float(n_sel), hkv=hkv, kc=kc)
    return pl.pallas_call(
        kern,
        grid=(seq // TQ,),
        in_specs=[
            pl.BlockSpec((TQ, d_a), lambda i: (i, 0)),
            pl.BlockSpec((TQ, IDX_HEADS * IDX_DIM), lambda i: (i, lay["off_qi"] // (IDX_HEADS * IDX_DIM))),
            pl.BlockSpec((LANES, TQ), lambda i: (0, i)),
            pl.BlockSpec((seq, nka), lambda i: (0, lay["off_ka"] // nka), pipeline_mode=one),
            pl.BlockSpec((seq, 2 * LANES), lambda i: (0, lay["off_misc"] // (2 * LANES)), pipeline_mode=one),
            pl.BlockSpec((seq // kc, hkv * VA_ROWS, kc), lambda i: (0, 0, 0), pipeline_mode=one),
            pl.BlockSpec((1, d_a), lambda i: (0, 0)),
        ],
        out_specs=pl.BlockSpec((TQ, d_a), lambda i: (i, 0)),
        out_shape=jax.ShapeDtypeStruct((seq, d_a), BF16),
        scratch_shapes=[
            pltpu.VMEM((seq, TQ), I32),
            pltpu.VMEM((seq, TQ), I16),
            pltpu.VMEM((seq, TQ), I16),
            pltpu.VMEM((hkv, VA_ROWS, A_GROUP * TQ), F32),
            pltpu.VMEM((hkv, 1, A_GROUP * TQ), F32),
        ],
        compiler_params=_params(("arbitrary",)),
        name="dsa",
    )(pr, pr, wt, pr, pr, vat3, g_a)


def _swa_kernel(qb_ref, kp_ref, kc_ref, vp_ref, vc_ref, se_ref, so_ref, g_ref, o_ref, ot_ref, *, hkv):
    i = pl.program_id(0)
    npair = B_GROUP // 2
    kwin = jnp.concatenate([kp_ref[...], kc_ref[...]], axis=0).astype(F32)
    r = lax.broadcasted_iota(I32, (2 * WINDOW, TQ), 0) - WINDOW
    q = lax.broadcasted_iota(I32, (2 * WINDOW, TQ), 1)
    ok = (r <= q) & (q - r < WINDOW) & (i * WINDOW + r >= 0)
    bias = jnp.where(ok, 0.0, NEG)
    bias4 = jnp.concatenate([bias] * npair, axis=1)
    lane = lax.broadcasted_iota(I32, (2 * WINDOW, LANES), 1)
    ssq = jnp.zeros((1, TQ), F32)
    for g in range(hkv):
        kcol = kwin[:, (g // 2) * LANES:(g // 2 + 1) * LANES]
        rolled = pltpu.roll(kcol, 64, 1)
        lo_src, hi_src = (kcol, rolled) if g % 2 == 0 else (rolled, kcol)
        k_even = jnp.where(lane < 64, lo_src, 0.0).astype(BF16)
        k_odd = jnp.where(lane >= 64, hi_src, 0.0).astype(BF16)
        qs = jnp.concatenate([qb_ref[:, (npair * g + p) * LANES:(npair * g + p + 1) * LANES]
                              for p in range(npair)], axis=0)
        vp = vp_ref[g * B_HEAD_DIM:(g + 1) * B_HEAD_DIM, :]
        vc = vc_ref[g * B_HEAD_DIM:(g + 1) * B_HEAD_DIM, :]
        outs = []
        for kmat, s_ref in ((k_even, se_ref), (k_odd, so_ref)):
            st = _dot_nt(kmat, qs) + bias4
            sink = s_ref[g]
            m = jnp.maximum(_col_reduce(jnp.maximum, jnp.max, st), sink)
            p = jnp.exp2(st - m)
            den = _col_reduce(jnp.add, jnp.sum, p) + jnp.exp2(sink - m)
            pb = p.astype(BF16)
            ot = (jnp.dot(vp, pb[:WINDOW], preferred_element_type=F32)
                  + jnp.dot(vc, pb[WINDOW:], preferred_element_type=F32))
            outs.append(ot / den)
        for p in range(npair):
            blk = jnp.concatenate([outs[0][:, p * TQ:(p + 1) * TQ], outs[1][:, p * TQ:(p + 1) * TQ]],
                                  axis=0)
            ot_ref[npair * g + p] = blk
            ssq = ssq + jnp.sum(blk * blk, axis=0, keepdims=True)
    rn = lax.rsqrt(ssq / (hkv * B_GROUP * B_HEAD_DIM) + EPS)
    for c in range(hkv * npair):
        o_ref[:, c * LANES:(c + 1) * LANES] = (
            (ot_ref[c] * rn).T * g_ref[:, c * LANES:(c + 1) * LANES]).astype(o_ref.dtype)


def _swa(pr, vbt3, sink_e, sink_o, g_b, lay, seq):
    d_b = lay["d_b"]
    hkv = lay["hkv_b"]
    kc = lay["kc"]
    per = kc // TQ
    nvb = hkv * B_HEAD_DIM
    off_kb = lay["off_misc"] + 2 * LANES
    assert nvb % LANES == 0 and off_kb % nvb == 0
    col_kb = off_kb // nvb
    prev = lambda i: jnp.maximum(i - 1, 0)
    kern = functools.partial(_swa_kernel, hkv=hkv)
    return pl.pallas_call(
        kern,
        grid=(seq // TQ,),
        in_specs=[
            pl.BlockSpec((TQ, d_b), lambda i: (i, lay["off_qb"] // d_b)),
            pl.BlockSpec((TQ, nvb), lambda i: (prev(i), col_kb)),
            pl.BlockSpec((TQ, nvb), lambda i: (i, col_kb)),
            pl.BlockSpec((None, nvb, TQ), lambda i: (prev(i) // per, 0, prev(i) % per)),
            pl.BlockSpec((None, nvb, TQ), lambda i: (i // per, 0, i % per)),
            pl.BlockSpec((hkv, 1, (B_GROUP // 2) * TQ), lambda i: (0, 0, 0)),
            pl.BlockSpec((hkv, 1, (B_GROUP // 2) * TQ), lambda i: (0, 0, 0)),
            pl.BlockSpec((1, d_b), lambda i: (0, 0)),
        ],
        out_specs=pl.BlockSpec((TQ, d_b), lambda i: (i, 0)),
        out_shape=jax.ShapeDtypeStruct((seq, d_b), BF16),
        scratch_shapes=[pltpu.VMEM((hkv * B_GROUP // 2, LANES, TQ), F32)],
        compiler_params=_params(("parallel",)),
        name="swa",
    )(pr, pr, pr, vbt3, vbt3, sink_e, sink_o, g_b)


def _oproj_kernel(na_ref, nb_ref, wa_ref, wb_ref, x_ref, gt_ref, o_ref):
    acc = jnp.dot(na_ref[...], wa_ref[...], preferred_element_type=F32)
    acc = acc + jnp.dot(nb_ref[...], wb_ref[...], preferred_element_type=F32)
    o_ref[...] = x_ref[...] + gt_ref[...] * acc


def _oproj(na, nb, w_out, x, gate):
    s, d_a = na.shape
    d_b = nb.shape[1]
    d = x.shape[1]
    assert d_a == d_b
    tm = _pick(s, (1024, 512, 256, 128))
    tn = _pick(d, (512, 256, 128))
    return pl.pallas_call(
        _oproj_kernel,
        grid=(s // tm, d // tn),
        in_specs=[
            pl.BlockSpec((tm, d_a), lambda i, j: (i, 0)),
            pl.BlockSpec((tm, d_b), lambda i, j: (i, 0)),
            pl.BlockSpec((d_a, tn), lambda i, j: (0, j)),
            pl.BlockSpec((d_b, tn), lambda i, j: (1, j)),
            pl.BlockSpec((tm, tn), lambda i, j: (i, j)),
            pl.BlockSpec((1, tn), lambda i, j: (0, j)),
        ],
        out_specs=pl.BlockSpec((tm, tn), lambda i, j: (i, j)),
        out_shape=jax.ShapeDtypeStruct((s, d), F32),
        compiler_params=_params(("parallel", "arbitrary")),
        name="oproj",
    )(na, nb, w_out, w_out, x, gate)


HALO = 8
UP_PARTS = 1
FF_ALIGN = 512


def _up_kernel(h_ref, wg_ref, wv_ref, cwg_ref, cwv_ref, cbg_ref, cbv_ref, o_ref, eg_ref, ev_ref, *, tm, parts):
    i = pl.program_id(1)

    @pl.when(i == 0)
    def _():
        eg_ref[0:HALO, :] = jnp.zeros((HALO, eg_ref.shape[1]), F32)
        ev_ref[0:HALO, :] = jnp.zeros((HALO, ev_ref.shape[1]), F32)

    @pl.when(i > 0)
    def _():
        eg_ref[0:HALO, :] = eg_ref[tm:tm + HALO, :]
        ev_ref[0:HALO, :] = ev_ref[tm:tm + HALO, :]

    def conv(e_ref, cw_ref, cb_ref, r0, n):
        y = cb_ref[...] + cw_ref[CONV_WIDTH - 1:CONV_WIDTH, :] * e_ref[HALO + r0:HALO + r0 + n, :]
        for k in range(1, CONV_WIDTH):
            y = y + cw_ref[CONV_WIDTH - 1 - k:CONV_WIDTH - k, :] * e_ref[HALO + r0 - k:HALO + r0 - k + n, :]
        return y

    n = tm // parts
    for p in range(parts):
        h = h_ref[p * n:(p + 1) * n, :]
        eg_ref[HALO + p * n:HALO + (p + 1) * n, :] = jnp.dot(h, wg_ref[...], preferred_element_type=F32)
        ev_ref[HALO + p * n:HALO + (p + 1) * n, :] = jnp.dot(h, wv_ref[...], preferred_element_type=F32)
    for p in range(parts):
        gte = conv(eg_ref, cwg_ref, cbg_ref, p * n, n)
        val = conv(ev_ref, cwv_ref, cbv_ref, p * n, n)
        o_ref[p * n:(p + 1) * n, :] = (gte * (1.0 / (1.0 + jnp.exp(-gte))) * val).astype(o_ref.dtype)


def _up(h, w_up, conv_w, conv_b):
    s, d = h.shape
    f = w_up.shape[1] // 2
    tn = _pick(f, (512, 256, 128))
    tm = _pick(s, (1024, 512, 256, 128))
    nf = f // tn
    kern = functools.partial(_up_kernel, tm=tm, parts=UP_PARTS if tm % (UP_PARTS * 128) == 0 else 1)
    gate = lambda j, i: (0, j)
    val = lambda j, i: (0, nf + j)
    return pl.pallas_call(
        kern,
        grid=(nf, s // tm),
        in_specs=[
            pl.BlockSpec((tm, d), lambda j, i: (i, 0)),
            pl.BlockSpec((d, tn), gate), pl.BlockSpec((d, tn), val),
            pl.BlockSpec((CONV_WIDTH, tn), gate), pl.BlockSpec((CONV_WIDTH, tn), val),
            pl.BlockSpec((1, tn), gate), pl.BlockSpec((1, tn), val),
        ],
        out_specs=pl.BlockSpec((tm, tn), lambda j, i: (i, j)),
        out_shape=jax.ShapeDtypeStruct((s, f), BF16),
        scratch_shapes=[pltpu.VMEM((tm + HALO, tn), F32), pltpu.VMEM((tm + HALO, tn), F32)],
        compiler_params=_params(("arbitrary", "arbitrary")),
        name="up_conv_gate",
    )(h, w_up, w_up, conv_w, conv_w, conv_b, conv_b)


def _down_kernel(a_ref, w_ref, x_ref, gt_ref, o_ref):
    o_ref[...] = x_ref[...] + gt_ref[...] * jnp.dot(a_ref[...], w_ref[...], preferred_element_type=F32)


def _down(act, wd, x, gate):
    s, f = act.shape
    d = x.shape[1]
    tm = _pick(s, (512, 256, 128))
    tn = _pick(d, (512, 256, 128))
    return pl.pallas_call(
        _down_kernel,
        grid=(s // tm, d // tn),
        in_specs=[
            pl.BlockSpec((tm, f), lambda i, j: (i, 0)),
            pl.BlockSpec((f, tn), lambda i, j: (0, j)),
            pl.BlockSpec((tm, tn), lambda i, j: (i, j)),
            pl.BlockSpec((1, tn), lambda i, j: (0, j)),
        ],
        out_specs=pl.BlockSpec((tm, tn), lambda i, j: (i, j)),
        out_shape=jax.ShapeDtypeStruct((s, d), F32),
        compiler_params=_params(("parallel", "arbitrary")),
        name="down",
    )(act, wd, x, gate)


def _layout(d_model, seq):
    d_a = d_model // 2
    d_b = d_model - d_a
    ha = d_a // A_HEAD_DIM
    hkv_a = ha // A_GROUP
    hb = d_b // B_HEAD_DIM
    hkv_b = hb // B_GROUP
    n_qi = IDX_HEADS * IDX_DIM
    n_ka = hkv_a * A_HEAD_DIM
    n_kb = hkv_b * B_HEAD_DIM
    misc_raw = 2 * LANES + n_kb
    seg_w = [d_a, d_b, n_qi, n_ka]
    tn = 512
    while any(w % tn for w in seg_w) or tn > misc_raw + LANES:
        tn //= 2
    misc_w = -(-misc_raw // tn) * tn
    off_qb = d_a
    off_qi = off_qb + d_b
    off_ka = off_qi + n_qi
    off_misc = off_ka + n_ka
    n_r = off_misc + misc_w
    bounds = (off_qb // tn, off_qi // tn, off_ka // tn, off_misc // tn)
    scales = (A_HEAD_DIM ** -0.5 * LOG2E, B_HEAD_DIM ** -0.5 * LOG2E, IDX_DIM ** -0.5)
    kc = min(KC, seq)
    assert d_a == d_b and seq % kc == 0 and kc % TQ == 0 and off_qi % n_qi == 0 and off_ka % n_ka == 0
    assert off_misc % (2 * LANES) == 0
    return dict(d_a=d_a, d_b=d_b, ha=ha, hkv_a=hkv_a, hb=hb, hkv_b=hkv_b, n_qi=n_qi, n_ka=n_ka, n_kb=n_kb,
                tn=tn, misc_w=misc_w, off_qb=off_qb, off_qi=off_qi, off_ka=off_ka, off_misc=off_misc,
                n_r=n_r, bounds=bounds, scales=scales, kc=kc, n_va=n_ka, n_vb=n_kb)


def _prep_w_in(w_in, lay):
    d_a, d_b = lay["d_a"], lay["d_b"]
    widths = (d_a, lay["n_ka"], lay["n_ka"], lay["n_qi"], IDX_DIM, IDX_HEADS, d_b, lay["n_kb"], lay["n_kb"])
    offs = np.concatenate([[0], np.cumsum(widths)])
    qa, ka, va, qi, ki, wi, qb, kb, vb = [w_in[..., offs[n]:offs[n + 1]] for n in range(9)]
    zk = jnp.zeros_like(ki)
    pad_m = lay["misc_w"] - (2 * LANES + lay["n_kb"])
    parts = [qa, qb, qi, ka, ki, zk, zk, ki, kb]
    if pad_m:
        parts.append(jnp.zeros(w_in.shape[:-1] + (pad_m,), w_in.dtype))
    w_r = jnp.concatenate(parts, axis=-1).astype(BF16)
    padw = jnp.zeros(w_in.shape[:-1] + (LANES - IDX_HEADS,), w_in.dtype)
    w_v = jnp.concatenate([va, vb, wi, padw], axis=-1).astype(BF16)
    return w_r, w_v


def _rope_tabs(positions):
    pos = positions[0].astype(F32)
    inv_a = ROPE_THETA ** (-jnp.arange(0, A_HEAD_DIM, 2, dtype=F32) / A_HEAD_DIM)
    ang_a = pos[:, None] * inv_a
    ca, sa = jnp.cos(ang_a), jnp.sin(ang_a)
    inv_6 = ROPE_THETA ** (-jnp.arange(0, B_HEAD_DIM, 2, dtype=F32) / B_HEAD_DIM)
    ang_6 = pos[:, None] * inv_6
    c6, s6 = jnp.cos(ang_6), jnp.sin(ang_6)
    z6 = jnp.zeros_like(s6)
    return jnp.concatenate([
        ca, ca, -sa, sa,
        c6, c6, c6, c6,
        z6, s6, z6, s6,
        -s6, z6, -s6, z6], axis=-1)


def kernel(x, c, positions, w_ada, b_ada, g_mix, w_in, g_out_a, g_out_b, sinks, w_out, g_ffn, w_up,
           conv_w, conv_b, w_down, g_final):
    b, seq, d = x.shape
    assert b == 1 and IDX_DIM == B_HEAD_DIM
    depth = w_ada.shape[0]
    lay = _layout(d, seq)
    n_sel = min(TOPK_MAX, seq // 4)
    hkv_b = lay["hkv_b"]

    w_r, w_v = _prep_w_in(w_in, lay)
    w_out_b = w_out.astype(BF16)
    f = w_down.shape[1]
    padf = -f % FF_ALIGN
    halves = lambda a: jnp.concatenate(
        [a[..., :f], jnp.zeros(a.shape[:-1] + (padf,), a.dtype), a[..., f:],
         jnp.zeros(a.shape[:-1] + (padf,), a.dtype)], axis=-1)
    w_up_b = halves(w_up).astype(BF16)
    w_down_b = jnp.concatenate([w_down, jnp.zeros((depth, padf, d), w_down.dtype)], axis=1).astype(BF16)
    conv_wp = halves(conv_w)
    conv_b3 = halves(conv_b)[:, None, :]
    tabs = _rope_tabs(positions)
    sk = (sinks * LOG2E).reshape(depth, hkv_b, B_GROUP // 2, 2)
    rep = lambda a: jnp.repeat(a, TQ, axis=-1).reshape(depth, hkv_b, 1, (B_GROUP // 2) * TQ)
    sink_e, sink_o = rep(sk[..., 0]), rep(sk[..., 1])

    mod = _ada(c.reshape(d, 1), w_ada, b_ada)
    xs = x[0]
    for l in range(depth):
        sh_m, sc_m, gt_m, sh_f, sc_f, gt_f = [mod[l, :, n * d:(n + 1) * d] for n in range(N_MOD)]
        h = _norm_mod(xs, g_mix[l][None, :], sc_m, sh_m)
        pr = _proj_rope(h, w_r[l], tabs, lay)
        vat3, vbt3, wt = _proj_v(h, w_v[l], lay["n_va"], lay["n_vb"], lay["kc"])
        na = _dsa(pr, vat3, wt, g_out_a[l][None, :], lay, seq, n_sel)
        nb = _swa(pr, vbt3, sink_e[l], sink_o[l], g_out_b[l][None, :], lay, seq)
        xs = _oproj(na, nb, w_out_b[l], xs, gt_m)
        h = _norm_mod(xs, g_ffn[l][None, :], sc_f, sh_f)
        act = _up(h, w_up_b[l], conv_wp[l], conv_b3[l])
        xs = _down(act, w_down_b[l], xs, gt_f)
    return _final_norm(xs, g_final[None, :])[None]
```

```python
import functools
import math

import numpy as np
import jax
import jax.numpy as jnp
from jax import lax
from jax.experimental import pallas as pl
from jax.experimental.pallas import tpu as pltpu

A_HEAD_DIM = 128
A_GROUP = 4
IDX_HEADS = 16
IDX_DIM = 64
TOPK_MAX = 256
B_HEAD_DIM = 64
B_GROUP = 8
WINDOW = 128
CONV_WIDTH = 3
ROPE_THETA = 10000.0
EPS = 1e-6
NEG = -1e30
M_INIT = -1e29
N_MOD = 6
LOG2E = 1.4426950408889634

LANES = 128
VMEM_LIMIT = 56 * 1024 * 1024

TQ = 128
KC = 512
COUNT_UNROLL = 4

F32 = jnp.float32
BF16 = jnp.bfloat16
I32 = jnp.int32
I16 = jnp.int16

_NEG_BITS = int(np.float32(NEG).view(np.int32))
KEY_NEG = _NEG_BITS ^ ((_NEG_BITS >> 31) & 0x7FFFFFFF)
NEG_HI = KEY_NEG >> 16
NEG_LO = (KEY_NEG & 0xFFFF) - 32768
I16_MIN = -32768


def _params(sem, vmem=VMEM_LIMIT):
    return pltpu.CompilerParams(dimension_semantics=sem, vmem_limit_bytes=vmem)


def _pick(n, prefs):
    for p in prefs:
        if n % p == 0:
            return p
    return n


def _dot_nt(a, b):
    return lax.dot_general(a, b, (((1,), (1,)), ((), ())), preferred_element_type=F32)


def _tree(op, parts):
    parts = list(parts)
    while len(parts) > 1:
        nxt = [op(parts[a], parts[a + 1]) for a in range(0, len(parts) - 1, 2)]
        if len(parts) % 2:
            nxt.append(parts[-1])
        parts = nxt
    return parts[0]


def _col_reduce(op, red, x):
    slabs = [x[r:r + 8] for r in range(0, x.shape[0], 8)]
    return red(_tree(op, slabs), axis=0, keepdims=True)


def _ada_kernel(c_ref, w_ref, b_ref, o_ref):
    k = pl.program_id(2)

    @pl.when(k == 0)
    def _():
        o_ref[...] = b_ref[...]

    c = c_ref[...]
    ca = c * (1.0 / (1.0 + jnp.exp(-c)))
    o_ref[...] += jnp.sum(w_ref[...] * ca, axis=0, keepdims=True)


def _ada(c_col, w_ada, b_ada):
    depth, d, n = w_ada.shape
    tk = _pick(d, (2048, 1024, 512, 256, 128))
    tn = _pick(n, (2048, 1024, 512, 256, 128))
    return pl.pallas_call(
        _ada_kernel,
        grid=(depth, n // tn, d // tk),
        in_specs=[
            pl.BlockSpec((tk, 1), lambda l, j, k: (k, 0)),
            pl.BlockSpec((None, tk, tn), lambda l, j, k: (l, k, j)),
            pl.BlockSpec((None, 1, tn), lambda l, j, k: (l, 0, j)),
        ],
        out_specs=pl.BlockSpec((None, 1, tn), lambda l, j, k: (l, 0, j)),
        out_shape=jax.ShapeDtypeStruct((depth, 1, n), F32),
        compiler_params=_params(("parallel", "parallel", "arbitrary")),
        name="ada",
    )(c_col, w_ada, b_ada.reshape(depth, 1, n))


def _norm_mod_kernel(x_ref, g_ref, sc_ref, sh_ref, o_ref):
    x = x_ref[...]
    ms = jnp.mean(x * x, axis=-1, keepdims=True)
    y = x * lax.rsqrt(ms + EPS) * g_ref[...]
    o_ref[...] = (y * (1.0 + sc_ref[...]) + sh_ref[...]).astype(o_ref.dtype)


def _norm_kernel(x_ref, g_ref, o_ref):
    x = x_ref[...]
    ms = jnp.mean(x * x, axis=-1, keepdims=True)
    o_ref[...] = (x * lax.rsqrt(ms + EPS) * g_ref[...]).astype(o_ref.dtype)


def _norm_mod(x, g, sc, sh):
    s, d = x.shape
    tm = _pick(s, (512, 256, 128))
    row = pl.BlockSpec((1, d), lambda i: (0, 0))
    return pl.pallas_call(
        _norm_mod_kernel,
        grid=(s // tm,),
        in_specs=[pl.BlockSpec((tm, d), lambda i: (i, 0)), row, row, row],
        out_specs=pl.BlockSpec((tm, d), lambda i: (i, 0)),
        out_shape=jax.ShapeDtypeStruct((s, d), BF16),
        compiler_params=_params(("parallel",)),
        name="norm_mod",
    )(x, g, sc, sh)


def _final_norm(x, g):
    s, d = x.shape
    tm = _pick(s, (512, 256, 128))
    return pl.pallas_call(
        _norm_kernel,
        grid=(s // tm,),
        in_specs=[pl.BlockSpec((tm, d), lambda i: (i, 0)), pl.BlockSpec((1, d), lambda i: (0, 0))],
        out_specs=pl.BlockSpec((tm, d), lambda i: (i, 0)),
        out_shape=jax.ShapeDtypeStruct((s, d), F32),
        compiler_params=_params(("parallel",)),
        name="final_norm",
    )(x, g)


def _proj_rope_kernel(h_ref, w_ref, tab_ref, o_ref, *, bounds, scales, tn, parts):
    j = pl.program_id(1)
    b_qa, b_qb, b_qi, b_ka = bounds
    s_qa, s_qb, s_qi = scales
    is_a = (j < b_qa) | ((j >= b_qi) & (j < b_ka))
    scale = jnp.where(j < b_qa, s_qa, jnp.where(j < b_qb, s_qb, jnp.where(j < b_qi, s_qi, 1.0))).astype(F32)
    n = h_ref.shape[0] // parts
    w = w_ref[...]

    def rope128(acc, r):
        c = tab_ref[r, 0:LANES] * scale
        s = tab_ref[r, LANES:2 * LANES] * scale
        for g in range(tn // LANES):
            xg = acc[:, g * LANES:(g + 1) * LANES]
            o_ref[r, g * LANES:(g + 1) * LANES] = (xg * c + pltpu.roll(xg, 64, 1) * s).astype(o_ref.dtype)

    def rope64(acc, r):
        c = tab_ref[r, 2 * LANES:3 * LANES] * scale
        shi = tab_ref[r, 3 * LANES:4 * LANES] * scale
        slo = tab_ref[r, 4 * LANES:5 * LANES] * scale
        for g in range(tn // LANES):
            xg = acc[:, g * LANES:(g + 1) * LANES]
            o_ref[r, g * LANES:(g + 1) * LANES] = (
                xg * c + pltpu.roll(xg, 32, 1) * shi + pltpu.roll(xg, 96, 1) * slo).astype(o_ref.dtype)

    def body(epilogue):
        for p in range(parts):
            r = slice(p * n, (p + 1) * n)
            epilogue(jnp.dot(h_ref[r, :], w, preferred_element_type=F32), r)

    pl.when(is_a)(lambda: body(rope128))
    pl.when(jnp.logical_not(is_a))(lambda: body(rope64))


def _proj_rope(h, w_r, layer, tabs, seg):
    s, d = h.shape
    n = w_r.shape[2]
    tn = seg["tn"]
    tm = _pick(s, (1024, 512, 256, 128))
    kern = functools.partial(_proj_rope_kernel, bounds=seg["bounds"], scales=seg["scales"], tn=tn,
                             parts=4 if tm % 512 == 0 else 1)
    return pl.pallas_call(
        kern,
        grid=(s // tm, n // tn),
        in_specs=[
            pl.BlockSpec((tm, d), lambda i, j: (i, 0)),
            pl.BlockSpec((None, d, tn), lambda i, j: (layer, 0, j)),
            pl.BlockSpec((tm, 5 * LANES), lambda i, j: (i, 0)),
        ],
        out_specs=pl.BlockSpec((tm, tn), lambda i, j: (i, j)),
        out_shape=jax.ShapeDtypeStruct((s, n), BF16),
        compiler_params=_params(("parallel", "arbitrary")),
        name="proj_rope",
    )(h, w_r, tabs)


VA_PAD = 16
VA_ROWS = A_HEAD_DIM + VA_PAD


def _proj_v_kernel(h_ref, w_ref, va_ref, vb_ref, wt_ref, *, n_va, n_vb, kc):
    acc = jnp.dot(h_ref[...], w_ref[...], preferred_element_type=F32)
    acct = acc.T
    tm = acc.shape[0]
    ones_rows = jnp.where(lax.broadcasted_iota(I32, (VA_PAD, kc), 0) == 0, 1.0, 0.0).astype(va_ref.dtype)
    for c in range(tm // kc):
        cols = slice(c * kc, (c + 1) * kc)
        for g in range(n_va // A_HEAD_DIM):
            va_ref[c, g * VA_ROWS:g * VA_ROWS + A_HEAD_DIM, :] = (
                acct[g * A_HEAD_DIM:(g + 1) * A_HEAD_DIM, cols].astype(va_ref.dtype))
            va_ref[c, g * VA_ROWS + A_HEAD_DIM:(g + 1) * VA_ROWS, :] = ones_rows
        vb_ref[c] = acct[n_va:n_va + n_vb, cols].astype(vb_ref.dtype)
    wt_ref[...] = acct[n_va + n_vb:n_va + n_vb + LANES] * (IDX_HEADS ** -0.5)


def _proj_v(h, w_v, layer, n_va, n_vb, kc):
    s, d = h.shape
    nv = w_v.shape[2]
    tm = _pick(s, (1024, 512))
    tm = max(tm, kc)
    n_va_aug = n_va // A_HEAD_DIM * VA_ROWS
    kern = functools.partial(_proj_v_kernel, n_va=n_va, n_vb=n_vb, kc=kc)
    return pl.pallas_call(
        kern,
        grid=(s // tm,),
        in_specs=[pl.BlockSpec((tm, d), lambda i: (i, 0)),
                  pl.BlockSpec((None, d, nv), lambda i: (layer, 0, 0))],
        out_specs=[
            pl.BlockSpec((tm // kc, n_va_aug, kc), lambda i: (i, 0, 0)),
            pl.BlockSpec((tm // kc, n_vb, kc), lambda i: (i, 0, 0)),
            pl.BlockSpec((LANES, tm), lambda i: (0, i)),
        ],
        out_shape=[
            jax.ShapeDtypeStruct((s // kc, n_va_aug, kc), BF16),
            jax.ShapeDtypeStruct((s // kc, n_vb, kc), BF16),
            jax.ShapeDtypeStruct((LANES, s), F32),
        ],
        compiler_params=_params(("parallel",)),
        name="proj_v",
    )(h, w_v)


def _dsa_kernel(qa_ref, qi_ref, wt_ref, ka_ref, ke_ref, vt_ref, g_ref, o_ref,
                key_ref, hi_ref, lo_ref, acc_ref, m_ref, st_ref, p_ref, al_ref, *, seq, n_sel, hkv, kc):
    i = pl.program_id(0)
    per = kc // TQ
    nchunk = (i + per) // per
    t_idx = i * TQ + lax.broadcasted_iota(I32, (1, TQ), 1)
    row_iota = lax.broadcasted_iota(I32, (kc, TQ), 0)
    n_pairs = IDX_HEADS // 2
    pairs_per_dot = 2
    n_dots = n_pairs // pairs_per_dot

    def rows(c):
        return pl.ds(c * kc if isinstance(c, int) else pl.multiple_of(c * kc, kc), kc)

    qp = [jnp.concatenate([qi_ref[:, (b * pairs_per_dot + p) * LANES:(b * pairs_per_dot + p + 1) * LANES]
                           for p in range(pairs_per_dot)], axis=0) for b in range(n_dots)]
    wrows = [wt_ref[h:h + 1, :] for h in range(IDX_HEADS)]

    def score_chunk(c, _):
        ke = ke_ref[rows(c), :]
        k_even, k_odd = ke[:, :LANES], ke[:, LANES:]
        logits = lambda b: (_dot_nt(k_even, qp[b]), _dot_nt(k_odd, qp[b]))
        cur = logits(0)
        sc = jnp.zeros((kc, TQ), F32)
        for b in range(n_dots):
            nxt = logits(b + 1) if b + 1 < n_dots else None
            for p in range(pairs_per_dot):
                h = 2 * (b * pairs_per_dot + p)
                sc = sc + wrows[h] * jnp.maximum(cur[0][:, p * TQ:(p + 1) * TQ], 0.0)
                sc = sc + wrows[h + 1] * jnp.maximum(cur[1][:, p * TQ:(p + 1) * TQ], 0.0)
            cur = nxt
        sc = jnp.where(c * kc + row_iota <= t_idx, sc, NEG)
        b32 = lax.bitcast_convert_type(sc, I32)
        key = b32 ^ (lax.shift_right_arithmetic(b32, 31) & 0x7FFFFFFF)
        key_ref[rows(c), :] = key
        hi_ref[rows(c), :] = lax.shift_right_arithmetic(key, 16).astype(I16)
        lo_ref[rows(c), :] = ((key & 0xFFFF) - 32768).astype(I16)
        return 0

    lax.fori_loop(0, nchunk, score_chunk, 0)

    unroll = math.gcd(COUNT_UNROLL, seq // kc)
    ntrip = (nchunk + unroll - 1) // unroll
    span = unroll * kc

    def fill_neg(c, _):
        hi_ref[rows(c), :] = jnp.full((kc, TQ), NEG_HI, I16)
        lo_ref[rows(c), :] = jnp.full((kc, TQ), NEG_LO, I16)
        return 0

    lax.fori_loop(nchunk, ntrip * unroll, fill_neg, 0)
    n_virtual = (seq - ntrip * span).astype(F32)
    cslab = 64

    def count16(ref, cand, strict):
        def body(t, acc):
            r0 = pl.multiple_of(t * span, span)
            for u in range(unroll):
                v = ref[pl.ds(r0 + u * kc, kc), :]
                hit = jnp.where((v > cand) if strict else (v >= cand), jnp.int16(1), jnp.int16(0))
                acc = acc + _tree(jnp.add, [hit[r:r + cslab] for r in range(0, kc, cslab)])
            return acc
        part = lax.fori_loop(0, ntrip, body, jnp.zeros((cslab, TQ), I16))
        return _col_reduce(jnp.add, jnp.sum, part.astype(F32))

    def bisect16(count_fn, cnt_all):
        def step(it, carry):
            u, cnt_u = carry
            cand_u = u | lax.shift_left(jnp.int32(1), jnp.int32(15) - it)
            cnt = count_fn(cand_u - 32768)
            ok = cnt >= n_sel
            return jnp.where(ok, cand_u, u), jnp.where(ok, cnt, cnt_u)
        return lax.fori_loop(0, 16, step, (jnp.zeros((1, TQ), I32), cnt_all))

    def count_hi_ge(cand):
        return count16(hi_ref, cand.astype(I16), False) + jnp.where(cand <= NEG_HI, n_virtual, 0.0)

    u_hi, cnt_hi_ge = bisect16(count_hi_ge, jnp.full((1, TQ), float(seq), F32))
    t_hi = u_hi - 32768
    t_hi16 = t_hi.astype(I16)
    cnt_hi_gt = count16(hi_ref, t_hi16, True) + jnp.where(t_hi < NEG_HI, n_virtual, 0.0)

    def mask_lo(t, _):
        r0 = pl.multiple_of(t * span, span)
        for u in range(unroll):
            r = pl.ds(r0 + u * kc, kc)
            lo_ref[r, :] = jnp.where(hi_ref[r, :] == t_hi16, lo_ref[r, :], jnp.int16(I16_MIN))
        return 0

    lax.fori_loop(0, ntrip, mask_lo, 0)
    virt_lo = jnp.where(t_hi == NEG_HI, n_virtual, 0.0)

    def count_lo_ge(cand):
        return (cnt_hi_gt + count16(lo_ref, cand.astype(I16), False)
                + jnp.where(cand <= NEG_LO, virt_lo, 0.0))

    u_lo, cnt_thr = bisect16(count_lo_ge, cnt_hi_ge)
    t_lo = u_lo - 32768
    cnt_gt = cnt_hi_gt + count16(lo_ref, t_lo.astype(I16), True) + jnp.where(t_lo < NEG_LO, virt_lo, 0.0)
    thr = lax.shift_left(t_hi, 16) | u_lo
    n_tie = n_sel - cnt_gt
    excess_ties = jnp.max(cnt_thr) > n_sel

    m_ref[...] = jnp.full(m_ref.shape, M_INIT, F32)
    acc_ref[...] = jnp.zeros(acc_ref.shape, F32)
    eye = jnp.where(lax.broadcasted_iota(I32, (TQ, LANES), 0) == lax.broadcasted_iota(I32, (TQ, LANES), 1),
                    1.0, 0.0).astype(BF16)
    rhs = [jnp.concatenate(
        [jnp.concatenate([qa_ref[:, (A_GROUP * g + h) * A_HEAD_DIM:(A_GROUP * g + h + 1) * A_HEAD_DIM], eye],
                         axis=1) for h in range(A_GROUP)], axis=0) for g in range(hkv)]

    def qk(c, bias_b, g):
        kg = ka_ref[rows(c), g * A_HEAD_DIM:(g + 1) * A_HEAD_DIM]
        return _dot_nt(jnp.concatenate([kg, bias_b], axis=1), rhs[g])

    def attend_chunk(c, bias_b, st, next_first):
        for g in range(hkv):
            if g + 1 < hkv:
                st_next = qk(c, bias_b, g + 1)
            else:
                st_next = next_first() if next_first is not None else None
            m_old = m_ref[g]
            m_new = jnp.maximum(m_old, _col_reduce(jnp.maximum, jnp.max, st))
            alpha = jnp.exp2(m_old - m_new)
            p = jnp.exp2(st - m_new).astype(BF16)
            m_ref[g] = m_new
            vt = vt_ref[c, g * VA_ROWS:(g + 1) * VA_ROWS, :]
            acc_ref[g] = acc_ref[g] * alpha + jnp.dot(vt, p, preferred_element_type=F32)
            st = st_next
        return st

    def plain_bias(c):
        return jnp.where(key_ref[rows(c), :] >= thr, 0.0, NEG).astype(BF16)

    def pv_update(c, g, p, alpha):
        vt = vt_ref[c, g * VA_ROWS:(g + 1) * VA_ROWS, :]
        acc_ref[g] = acc_ref[g] * alpha + jnp.dot(vt, p, preferred_element_type=F32)

    def attend_plain(c, _):
        st, p_prev, alpha_prev = st_ref[...], p_ref[...], al_ref[...]
        c_next = jnp.minimum(c + 1, nchunk - 1)
        c_prev = jnp.maximum(c - 1, 0)
        bias_b = plain_bias(c)
        for g in range(hkv):
            st_next = qk(c, bias_b, g + 1) if g + 1 < hkv else qk(c_next, plain_bias(c_next), 0)
            if g == 0:
                pv_update(c_prev, hkv - 1, p_prev, alpha_prev)
            else:
                pv_update(c, g - 1, p_prev, alpha_prev)
            m_old = m_ref[g]
            m_new = jnp.maximum(m_old, _col_reduce(jnp.maximum, jnp.max, st))
            alpha_prev = jnp.exp2(m_old - m_new)
            p_prev = jnp.exp2(st - m_new).astype(BF16)
            m_ref[g] = m_new
            st = st_next
        st_ref[...] = st
        p_ref[...] = p_prev
        al_ref[...] = alpha_prev
        return 0

    def attend_ties(c, tie_seen):
        k = key_ref[rows(c), :]
        ltri = jnp.where(lax.broadcasted_iota(I32, (kc, kc), 0) > lax.broadcasted_iota(I32, (kc, kc), 1),
                         1.0, 0.0).astype(BF16)
        eq = jnp.where(k == thr, 1.0, 0.0)
        rank = jnp.dot(ltri, eq.astype(BF16), preferred_element_type=F32) + tie_seen
        keep = jnp.where(k > thr, 1.0, jnp.where(rank < n_tie, eq, 0.0))
        keep = jnp.where(c * kc + row_iota <= t_idx, keep, 0.0)
        bias_b = jnp.where(keep > 0.5, 0.0, NEG).astype(BF16)
        attend_chunk(c, bias_b, qk(c, bias_b, 0), None)
        return tie_seen + _col_reduce(jnp.add, jnp.sum, eq)

    def run_ties():
        lax.fori_loop(0, nchunk, attend_ties, jnp.zeros((1, TQ), F32))

    def run_plain():
        st_ref[...] = qk(0, plain_bias(0), 0)
        p_ref[...] = jnp.zeros(p_ref.shape, BF16)
        al_ref[...] = jnp.ones(al_ref.shape, F32)
        lax.fori_loop(0, nchunk, attend_plain, 0)
        pv_update(nchunk - 1, hkv - 1, p_ref[...], al_ref[...])

    lax.cond(excess_ties, run_ties, run_plain)

    ssq = jnp.zeros((1, TQ), F32)
    for g in range(hkv):
        o = acc_ref[g, 0:A_HEAD_DIM, :] / acc_ref[g, A_HEAD_DIM:A_HEAD_DIM + 1, :]
        acc_ref[g, 0:A_HEAD_DIM, :] = o
        sq = jnp.sum(o * o, axis=0, keepdims=True)
        for h in range(A_GROUP):
            ssq = ssq + sq[:, h * TQ:(h + 1) * TQ]
    rn = lax.rsqrt(ssq / (hkv * A_GROUP * A_HEAD_DIM) + EPS)
    for g in range(hkv):
        for h in range(A_GROUP):
            col = (A_GROUP * g + h) * A_HEAD_DIM
            oh = acc_ref[g, 0:A_HEAD_DIM, h * TQ:(h + 1) * TQ] * rn
            o_ref[:, col:col + A_HEAD_DIM] = (oh.T * g_ref[:, col:col + A_HEAD_DIM]).astype(o_ref.dtype)


def _dsa(pr, vat3, wt, g_a, lay, seq, n_sel):
    d_a = lay["d_a"]
    hkv = lay["hkv_a"]
    nka = hkv * A_HEAD_DIM
    kc = lay["kc"]
    assert TQ == LANES
    one = pl.Buffered(1)
    kern = functools.partial(_dsa_kernel, seq=seq, n_sel=float(n_sel), hkv=hkv, kc=kc)
    return pl.pallas_call(
        kern,
        grid=(seq // TQ,),
        in_specs=[
            pl.BlockSpec((TQ, d_a), lambda i: (i, 0)),
            pl.BlockSpec((TQ, IDX_HEADS * IDX_DIM), lambda i: (i, lay["off_qi"] // (IDX_HEADS * IDX_DIM))),
            pl.BlockSpec((LANES, TQ), lambda i: (0, i)),
            pl.BlockSpec((seq, nka), lambda i: (0, lay["off_ka"] // nka), pipeline_mode=one),
            pl.BlockSpec((seq, 2 * LANES), lambda i: (0, lay["off_misc"] // (2 * LANES)), pipeline_mode=one),
            pl.BlockSpec((seq // kc, hkv * VA_ROWS, kc), lambda i: (0, 0, 0), pipeline_mode=one),
            pl.BlockSpec((1, d_a), lambda i: (0, 0)),
        ],
        out_specs=pl.BlockSpec((TQ, d_a), lambda i: (i, 0)),
        out_shape=jax.ShapeDtypeStruct((seq, d_a), BF16),
        scratch_shapes=[
            pltpu.VMEM((seq, TQ), I32),
            pltpu.VMEM((seq, TQ), I16),
            pltpu.VMEM((seq, TQ), I16),
            pltpu.VMEM((hkv, VA_ROWS, A_GROUP * TQ), F32),
            pltpu.VMEM((hkv, 1, A_GROUP * TQ), F32),
            pltpu.VMEM((kc, A_GROUP * TQ), F32),
            pltpu.VMEM((kc, A_GROUP * TQ), BF16),
            pltpu.VMEM((1, A_GROUP * TQ), F32),
        ],
        compiler_params=_params(("arbitrary",)),
        name="dsa",
    )(pr, pr, wt, pr, pr, vat3, g_a)


def _swa_kernel(qb_ref, kp_ref, kc_ref, vp_ref, vc_ref, se_ref, so_ref, g_ref, o_ref, ot_ref, *, hkv):
    i = pl.program_id(0)
    npair = B_GROUP // 2
    kwin = jnp.concatenate([kp_ref[...], kc_ref[...]], axis=0).astype(F32)
    r = lax.broadcasted_iota(I32, (2 * WINDOW, TQ), 0) - WINDOW
    q = lax.broadcasted_iota(I32, (2 * WINDOW, TQ), 1)
    ok = (r <= q) & (q - r < WINDOW) & (i * WINDOW + r >= 0)
    bias = jnp.where(ok, 0.0, NEG)
    bias4 = jnp.concatenate([bias] * npair, axis=1)
    lane = lax.broadcasted_iota(I32, (2 * WINDOW, LANES), 1)
    ssq = jnp.zeros((1, TQ), F32)
    for g in range(hkv):
        kcol = kwin[:, (g // 2) * LANES:(g // 2 + 1) * LANES]
        rolled = pltpu.roll(kcol, 64, 1)
        lo_src, hi_src = (kcol, rolled) if g % 2 == 0 else (rolled, kcol)
        k_even = jnp.where(lane < 64, lo_src, 0.0).astype(BF16)
        k_odd = jnp.where(lane >= 64, hi_src, 0.0).astype(BF16)
        qs = jnp.concatenate([qb_ref[:, (npair * g + p) * LANES:(npair * g + p + 1) * LANES]
                              for p in range(npair)], axis=0)
        vp = vp_ref[g * B_HEAD_DIM:(g + 1) * B_HEAD_DIM, :]
        vc = vc_ref[g * B_HEAD_DIM:(g + 1) * B_HEAD_DIM, :]
        outs = []
        for kmat, s_ref in ((k_even, se_ref), (k_odd, so_ref)):
            st = _dot_nt(kmat, qs) + bias4
            sink = s_ref[g]
            m = jnp.maximum(_col_reduce(jnp.maximum, jnp.max, st), sink)
            p = jnp.exp2(st - m)
            den = _col_reduce(jnp.add, jnp.sum, p) + jnp.exp2(sink - m)
            pb = p.astype(BF16)
            ot = (jnp.dot(vp, pb[:WINDOW], preferred_element_type=F32)
                  + jnp.dot(vc, pb[WINDOW:], preferred_element_type=F32))
            outs.append(ot / den)
        for p in range(npair):
            blk = jnp.concatenate([outs[0][:, p * TQ:(p + 1) * TQ], outs[1][:, p * TQ:(p + 1) * TQ]],
                                  axis=0)
            ot_ref[npair * g + p] = blk
            ssq = ssq + jnp.sum(blk * blk, axis=0, keepdims=True)
    rn = lax.rsqrt(ssq / (hkv * B_GROUP * B_HEAD_DIM) + EPS)
    for c in range(hkv * npair):
        o_ref[:, c * LANES:(c + 1) * LANES] = (
            (ot_ref[c] * rn).T * g_ref[:, c * LANES:(c + 1) * LANES]).astype(o_ref.dtype)


def _swa(pr, vbt3, sink_e, sink_o, g_b, lay, seq):
    d_b = lay["d_b"]
    hkv = lay["hkv_b"]
    kc = lay["kc"]
    per = kc // TQ
    nvb = hkv * B_HEAD_DIM
    off_kb = lay["off_misc"] + 2 * LANES
    assert nvb % LANES == 0 and off_kb % nvb == 0
    col_kb = off_kb // nvb
    prev = lambda i: jnp.maximum(i - 1, 0)
    kern = functools.partial(_swa_kernel, hkv=hkv)
    return pl.pallas_call(
        kern,
        grid=(seq // TQ,),
        in_specs=[
            pl.BlockSpec((TQ, d_b), lambda i: (i, lay["off_qb"] // d_b)),
            pl.BlockSpec((TQ, nvb), lambda i: (prev(i), col_kb)),
            pl.BlockSpec((TQ, nvb), lambda i: (i, col_kb)),
            pl.BlockSpec((None, nvb, TQ), lambda i: (prev(i) // per, 0, prev(i) % per)),
            pl.BlockSpec((None, nvb, TQ), lambda i: (i // per, 0, i % per)),
            pl.BlockSpec((hkv, 1, (B_GROUP // 2) * TQ), lambda i: (0, 0, 0)),
            pl.BlockSpec((hkv, 1, (B_GROUP // 2) * TQ), lambda i: (0, 0, 0)),
            pl.BlockSpec((1, d_b), lambda i: (0, 0)),
        ],
        out_specs=pl.BlockSpec((TQ, d_b), lambda i: (i, 0)),
        out_shape=jax.ShapeDtypeStruct((seq, d_b), BF16),
        scratch_shapes=[pltpu.VMEM((hkv * B_GROUP // 2, LANES, TQ), F32)],
        compiler_params=_params(("parallel",)),
        name="swa",
    )(pr, pr, pr, vbt3, vbt3, sink_e, sink_o, g_b)


def _oproj_kernel(na_ref, nb_ref, wa_ref, wb_ref, x_ref, gt_ref, o_ref):
    acc = jnp.dot(na_ref[...], wa_ref[...], preferred_element_type=F32)
    acc = acc + jnp.dot(nb_ref[...], wb_ref[...], preferred_element_type=F32)
    o_ref[...] = x_ref[...] + gt_ref[...] * acc


def _oproj(na, nb, w_out, layer, x, gate):
    s, d_a = na.shape
    d_b = nb.shape[1]
    d = x.shape[1]
    assert d_a == d_b
    tm = _pick(s, (1024, 512, 256, 128))
    tn = _pick(d, (512, 256, 128))
    return pl.pallas_call(
        _oproj_kernel,
        grid=(s // tm, d // tn),
        in_specs=[
            pl.BlockSpec((tm, d_a), lambda i, j: (i, 0)),
            pl.BlockSpec((tm, d_b), lambda i, j: (i, 0)),
            pl.BlockSpec((None, d_a, tn), lambda i, j: (layer, 0, j)),
            pl.BlockSpec((None, d_b, tn), lambda i, j: (layer, 1, j)),
            pl.BlockSpec((tm, tn), lambda i, j: (i, j)),
            pl.BlockSpec((1, tn), lambda i, j: (0, j)),
        ],
        out_specs=pl.BlockSpec((tm, tn), lambda i, j: (i, j)),
        out_shape=jax.ShapeDtypeStruct((s, d), F32),
        compiler_params=_params(("parallel", "arbitrary")),
        name="oproj",
    )(na, nb, w_out, w_out, x, gate)


HALO = 8
UP_PARTS = 1
FF_ALIGN = 512


def _up_kernel(h_ref, wg_ref, wv_ref, cwg_ref, cwv_ref, cbg_ref, cbv_ref, o_ref, eg_ref, ev_ref, *, tm, parts):
    i = pl.program_id(1)

    @pl.when(i == 0)
    def _():
        eg_ref[0:HALO, :] = jnp.zeros((HALO, eg_ref.shape[1]), F32)
        ev_ref[0:HALO, :] = jnp.zeros((HALO, ev_ref.shape[1]), F32)

    @pl.when(i > 0)
    def _():
        eg_ref[0:HALO, :] = eg_ref[tm:tm + HALO, :]
        ev_ref[0:HALO, :] = ev_ref[tm:tm + HALO, :]

    def conv(e_ref, cw_ref, cb_ref, r0, n):
        y = cb_ref[...] + cw_ref[CONV_WIDTH - 1:CONV_WIDTH, :] * e_ref[HALO + r0:HALO + r0 + n, :]
        for k in range(1, CONV_WIDTH):
            y = y + cw_ref[CONV_WIDTH - 1 - k:CONV_WIDTH - k, :] * e_ref[HALO + r0 - k:HALO + r0 - k + n, :]
        return y

    n = tm // parts
    for p in range(parts):
        h = h_ref[p * n:(p + 1) * n, :]
        eg_ref[HALO + p * n:HALO + (p + 1) * n, :] = jnp.dot(h, wg_ref[...], preferred_element_type=F32)
        ev_ref[HALO + p * n:HALO + (p + 1) * n, :] = jnp.dot(h, wv_ref[...], preferred_element_type=F32)
    for p in range(parts):
        gte = conv(eg_ref, cwg_ref, cbg_ref, p * n, n)
        val = conv(ev_ref, cwv_ref, cbv_ref, p * n, n)
        o_ref[p * n:(p + 1) * n, :] = (gte * (1.0 / (1.0 + jnp.exp(-gte))) * val).astype(o_ref.dtype)


def _up(h, w_up, layer, conv_w, conv_b):
    s, d = h.shape
    f = w_up.shape[2] // 2
    tn = _pick(f, (512, 256, 128))
    tm = _pick(s, (1024, 512, 256, 128))
    nf = f // tn
    kern = functools.partial(_up_kernel, tm=tm, parts=UP_PARTS if tm % (UP_PARTS * 128) == 0 else 1)
    gate = lambda j, i: (layer, 0, j)
    val = lambda j, i: (layer, 0, nf + j)
    return pl.pallas_call(
        kern,
        grid=(nf, s // tm),
        in_specs=[
            pl.BlockSpec((tm, d), lambda j, i: (i, 0)),
            pl.BlockSpec((None, d, tn), gate), pl.BlockSpec((None, d, tn), val),
            pl.BlockSpec((None, CONV_WIDTH, tn), gate), pl.BlockSpec((None, CONV_WIDTH, tn), val),
            pl.BlockSpec((None, 1, tn), gate), pl.BlockSpec((None, 1, tn), val),
        ],
        out_specs=pl.BlockSpec((tm, tn), lambda j, i: (i, j)),
        out_shape=jax.ShapeDtypeStruct((s, f), BF16),
        scratch_shapes=[pltpu.VMEM((tm + HALO, tn), F32), pltpu.VMEM((tm + HALO, tn), F32)],
        compiler_params=_params(("arbitrary", "arbitrary")),
        name="up_conv_gate",
    )(h, w_up, w_up, conv_w, conv_w, conv_b, conv_b)


def _down_kernel(a_ref, w_ref, x_ref, gt_ref, o_ref):
    o_ref[...] = x_ref[...] + gt_ref[...] * jnp.dot(a_ref[...], w_ref[...], preferred_element_type=F32)


def _down(act, wd, layer, x, gate):
    s, f = act.shape
    d = x.shape[1]
    tm = _pick(s, (512, 256, 128))
    tn = _pick(d, (512, 256, 128))
    return pl.pallas_call(
        _down_kernel,
        grid=(s // tm, d // tn),
        in_specs=[
            pl.BlockSpec((tm, f), lambda i, j: (i, 0)),
            pl.BlockSpec((None, f, tn), lambda i, j: (layer, 0, j)),
            pl.BlockSpec((tm, tn), lambda i, j: (i, j)),
            pl.BlockSpec((1, tn), lambda i, j: (0, j)),
        ],
        out_specs=pl.BlockSpec((tm, tn), lambda i, j: (i, j)),
        out_shape=jax.ShapeDtypeStruct((s, d), F32),
        compiler_params=_params(("parallel", "arbitrary")),
        name="down",
    )(act, wd, x, gate)


def _cast_pad_kernel(x_ref, o_ref, *, axis, n_real, n_out):
    r = pl.program_id(axis) % n_out

    @pl.when(r < n_real)
    def _():
        o_ref[...] = x_ref[...].astype(o_ref.dtype)

    @pl.when(r >= n_real)
    def _():
        o_ref[...] = jnp.zeros(o_ref.shape, o_ref.dtype)


def _pad_unit(f, padf):
    g = np.gcd(f, padf) if padf else f
    return _pick(int(g), (512, 256, 128))


def _cast_up(w_up, f, padf):
    depth, d, _ = w_up.shape
    cw = _pad_unit(f, padf)
    n_real, n_out = f // cw, (f + padf) // cw
    src = lambda l, t: (l, 0, (t // n_out) * n_real + jnp.minimum(t % n_out, n_real - 1))
    kern = functools.partial(_cast_pad_kernel, axis=1, n_real=n_real, n_out=n_out)
    return pl.pallas_call(
        kern,
        grid=(depth, 2 * n_out),
        in_specs=[pl.BlockSpec((None, d, cw), src)],
        out_specs=pl.BlockSpec((None, d, cw), lambda l, t: (l, 0, t)),
        out_shape=jax.ShapeDtypeStruct((depth, d, 2 * (f + padf)), BF16),
        compiler_params=_params(("parallel", "arbitrary")),
        name="cast_up",
    )(w_up)


def _cast_down(w_down, padf):
    depth, f, d = w_down.shape
    rt = _pad_unit(f, padf)
    n_real, n_out = f // rt, (f + padf) // rt
    kern = functools.partial(_cast_pad_kernel, axis=1, n_real=n_real, n_out=n_out)
    return pl.pallas_call(
        kern,
        grid=(depth, n_out),
        in_specs=[pl.BlockSpec((None, rt, d), lambda l, t: (l, jnp.minimum(t, n_real - 1), 0))],
        out_specs=pl.BlockSpec((None, rt, d), lambda l, t: (l, t, 0)),
        out_shape=jax.ShapeDtypeStruct((depth, f + padf, d), BF16),
        compiler_params=_params(("parallel", "arbitrary")),
        name="cast_down",
    )(w_down)


def _layout(d_model, seq):
    d_a = d_model // 2
    d_b = d_model - d_a
    ha = d_a // A_HEAD_DIM
    hkv_a = ha // A_GROUP
    hb = d_b // B_HEAD_DIM
    hkv_b = hb // B_GROUP
    n_qi = IDX_HEADS * IDX_DIM
    n_ka = hkv_a * A_HEAD_DIM
    n_kb = hkv_b * B_HEAD_DIM
    misc_raw = 2 * LANES + n_kb
    seg_w = [d_a, d_b, n_qi, n_ka]
    tn = 512
    while any(w % tn for w in seg_w) or tn > misc_raw + LANES:
        tn //= 2
    misc_w = -(-misc_raw // tn) * tn
    off_qb = d_a
    off_qi = off_qb + d_b
    off_ka = off_qi + n_qi
    off_misc = off_ka + n_ka
    n_r = off_misc + misc_w
    bounds = (off_qb // tn, off_qi // tn, off_ka // tn, off_misc // tn)
    scales = (A_HEAD_DIM ** -0.5 * LOG2E, B_HEAD_DIM ** -0.5 * LOG2E, IDX_DIM ** -0.5)
    kc = min(KC, seq)
    assert d_a == d_b and seq % kc == 0 and kc % TQ == 0 and off_qi % n_qi == 0 and off_ka % n_ka == 0
    assert off_misc % (2 * LANES) == 0
    return dict(d_a=d_a, d_b=d_b, ha=ha, hkv_a=hkv_a, hb=hb, hkv_b=hkv_b, n_qi=n_qi, n_ka=n_ka, n_kb=n_kb,
                tn=tn, misc_w=misc_w, off_qb=off_qb, off_qi=off_qi, off_ka=off_ka, off_misc=off_misc,
                n_r=n_r, bounds=bounds, scales=scales, kc=kc, n_va=n_ka, n_vb=n_kb)


def _prep_w_in(w_in, lay):
    d_a, d_b = lay["d_a"], lay["d_b"]
    widths = (d_a, lay["n_ka"], lay["n_ka"], lay["n_qi"], IDX_DIM, IDX_HEADS, d_b, lay["n_kb"], lay["n_kb"])
    offs = np.concatenate([[0], np.cumsum(widths)])
    qa, ka, va, qi, ki, wi, qb, kb, vb = [w_in[..., offs[n]:offs[n + 1]] for n in range(9)]
    zk = jnp.zeros_like(ki)
    pad_m = lay["misc_w"] - (2 * LANES + lay["n_kb"])
    parts = [qa, qb, qi, ka, ki, zk, zk, ki, kb]
    if pad_m:
        parts.append(jnp.zeros(w_in.shape[:-1] + (pad_m,), w_in.dtype))
    w_r = jnp.concatenate(parts, axis=-1).astype(BF16)
    padw = jnp.zeros(w_in.shape[:-1] + (LANES - IDX_HEADS,), w_in.dtype)
    w_v = jnp.concatenate([va, vb, wi, padw], axis=-1).astype(BF16)
    return w_r, w_v


def _rope_tabs(positions):
    pos = positions[0].astype(F32)
    inv_a = ROPE_THETA ** (-jnp.arange(0, A_HEAD_DIM, 2, dtype=F32) / A_HEAD_DIM)
    ang_a = pos[:, None] * inv_a
    ca, sa = jnp.cos(ang_a), jnp.sin(ang_a)
    inv_6 = ROPE_THETA ** (-jnp.arange(0, B_HEAD_DIM, 2, dtype=F32) / B_HEAD_DIM)
    ang_6 = pos[:, None] * inv_6
    c6, s6 = jnp.cos(ang_6), jnp.sin(ang_6)
    z6 = jnp.zeros_like(s6)
    return jnp.concatenate([
        ca, ca, -sa, sa,
        c6, c6, c6, c6,
        z6, s6, z6, s6,
        -s6, z6, -s6, z6], axis=-1)


def kernel(x, c, positions, w_ada, b_ada, g_mix, w_in, g_out_a, g_out_b, sinks, w_out, g_ffn, w_up,
           conv_w, conv_b, w_down, g_final):
    b, seq, d = x.shape
    assert b == 1 and IDX_DIM == B_HEAD_DIM
    depth = w_ada.shape[0]
    lay = _layout(d, seq)
    n_sel = min(TOPK_MAX, seq // 4)
    hkv_b = lay["hkv_b"]

    w_r, w_v = _prep_w_in(w_in, lay)
    w_out_b = w_out.astype(BF16)
    f = w_down.shape[1]
    padf = -f % FF_ALIGN
    halves = lambda a: jnp.concatenate(
        [a[..., :f], jnp.zeros(a.shape[:-1] + (padf,), a.dtype), a[..., f:],
         jnp.zeros(a.shape[:-1] + (padf,), a.dtype)], axis=-1)
    w_up_b = _cast_up(w_up, f, padf)
    w_down_b = _cast_down(w_down, padf)
    conv_wp = halves(conv_w)
    conv_b3 = halves(conv_b)[:, None, :]
    tabs = _rope_tabs(positions)
    sk = (sinks * LOG2E).reshape(depth, hkv_b, B_GROUP // 2, 2)
    rep = lambda a: jnp.repeat(a, TQ, axis=-1).reshape(depth, hkv_b, 1, (B_GROUP // 2) * TQ)
    sink_e, sink_o = rep(sk[..., 0]), rep(sk[..., 1])

    mod = _ada(c.reshape(d, 1), w_ada, b_ada)
    xs = x[0]
    for l in range(depth):
        sh_m, sc_m, gt_m, sh_f, sc_f, gt_f = [mod[l, :, n * d:(n + 1) * d] for n in range(N_MOD)]
        h = _norm_mod(xs, g_mix[l][None, :], sc_m, sh_m)
        pr = _proj_rope(h, w_r, l, tabs, lay)
        vat3, vbt3, wt = _proj_v(h, w_v, l, lay["n_va"], lay["n_vb"], lay["kc"])
        na = _dsa(pr, vat3, wt, g_out_a[l][None, :], lay, seq, n_sel)
        nb = _swa(pr, vbt3, sink_e[l], sink_o[l], g_out_b[l][None, :], lay, seq)
        xs = _oproj(na, nb, w_out_b, l, xs, gt_m)
        h = _norm_mod(xs, g_ffn[l][None, :], sc_f, sh_f)
        act = _up(h, w_up_b, l, conv_wp, conv_b3)
        xs = _down(act, w_down_b, l, xs, gt_f)
    return _final_norm(xs, g_final[None, :])[None]
```

```python
import functools
import math

import numpy as np
import jax
import jax.numpy as jnp
from jax import lax
from jax.experimental import pallas as pl
from jax.experimental.pallas import tpu as pltpu

A_HEAD_DIM = 128
A_GROUP = 4
IDX_HEADS = 16
IDX_DIM = 64
TOPK_MAX = 256
B_HEAD_DIM = 64
B_GROUP = 8
WINDOW = 128
CONV_WIDTH = 3
ROPE_THETA = 10000.0
EPS = 1e-6
NEG = -1e30
M_INIT = -1e29
N_MOD = 6
LOG2E = 1.4426950408889634

LANES = 128
VMEM_LIMIT = 56 * 1024 * 1024

TQ = 128
KC = 512
COUNT_UNROLL = 2

F32 = jnp.float32
BF16 = jnp.bfloat16
I32 = jnp.int32
I16 = jnp.int16

_NEG_BITS = int(np.float32(NEG).view(np.int32))
KEY_NEG = _NEG_BITS ^ ((_NEG_BITS >> 31) & 0x7FFFFFFF)
NEG_HI = KEY_NEG >> 16
NEG_LO = (KEY_NEG & 0xFFFF) - 32768
I16_MIN = -32768


def _params(sem, vmem=VMEM_LIMIT):
    return pltpu.CompilerParams(dimension_semantics=sem, vmem_limit_bytes=vmem)


def _pick(n, prefs):
    for p in prefs:
        if n % p == 0:
            return p
    return n


def _dot_nt(a, b):
    return lax.dot_general(a, b, (((1,), (1,)), ((), ())), preferred_element_type=F32)


def _tree(op, parts):
    parts = list(parts)
    while len(parts) > 1:
        nxt = [op(parts[a], parts[a + 1]) for a in range(0, len(parts) - 1, 2)]
        if len(parts) % 2:
            nxt.append(parts[-1])
        parts = nxt
    return parts[0]


def _col_reduce(op, red, x):
    slabs = [x[r:r + 8] for r in range(0, x.shape[0], 8)]
    return red(_tree(op, slabs), axis=0, keepdims=True)


def _ada_kernel(c_ref, w_ref, b_ref, o_ref):
    k = pl.program_id(2)

    @pl.when(k == 0)
    def _():
        o_ref[...] = b_ref[...]

    c = c_ref[...]
    ca = c * (1.0 / (1.0 + jnp.exp(-c)))
    o_ref[...] += jnp.sum(w_ref[...] * ca, axis=0, keepdims=True)


def _ada(c_col, w_ada, b_ada):
    depth, d, n = w_ada.shape
    tk = _pick(d, (2048, 1024, 512, 256, 128))
    tn = _pick(n, (2048, 1024, 512, 256, 128))
    return pl.pallas_call(
        _ada_kernel,
        grid=(depth, n // tn, d // tk),
        in_specs=[
            pl.BlockSpec((tk, 1), lambda l, j, k: (k, 0)),
            pl.BlockSpec((None, tk, tn), lambda l, j, k: (l, k, j)),
            pl.BlockSpec((None, 1, tn), lambda l, j, k: (l, 0, j)),
        ],
        out_specs=pl.BlockSpec((None, 1, tn), lambda l, j, k: (l, 0, j)),
        out_shape=jax.ShapeDtypeStruct((depth, 1, n), F32),
        compiler_params=_params(("parallel", "parallel", "arbitrary")),
        name="ada",
    )(c_col, w_ada, b_ada.reshape(depth, 1, n))


def _norm_mod_kernel(x_ref, g_ref, sc_ref, sh_ref, o_ref):
    x = x_ref[...]
    ms = jnp.mean(x * x, axis=-1, keepdims=True)
    y = x * lax.rsqrt(ms + EPS) * g_ref[...]
    o_ref[...] = (y * (1.0 + sc_ref[...]) + sh_ref[...]).astype(o_ref.dtype)


def _norm_kernel(x_ref, g_ref, o_ref):
    x = x_ref[...]
    ms = jnp.mean(x * x, axis=-1, keepdims=True)
    o_ref[...] = (x * lax.rsqrt(ms + EPS) * g_ref[...]).astype(o_ref.dtype)


def _norm_mod(x, g, sc, sh):
    s, d = x.shape
    tm = _pick(s, (512, 256, 128))
    row = pl.BlockSpec((1, d), lambda i: (0, 0))
    return pl.pallas_call(
        _norm_mod_kernel,
        grid=(s // tm,),
        in_specs=[pl.BlockSpec((tm, d), lambda i: (i, 0)), row, row, row],
        out_specs=pl.BlockSpec((tm, d), lambda i: (i, 0)),
        out_shape=jax.ShapeDtypeStruct((s, d), BF16),
        compiler_params=_params(("parallel",)),
        name="norm_mod",
    )(x, g, sc, sh)


def _final_norm(x, g):
    s, d = x.shape
    tm = _pick(s, (512, 256, 128))
    return pl.pallas_call(
        _norm_kernel,
        grid=(s // tm,),
        in_specs=[pl.BlockSpec((tm, d), lambda i: (i, 0)), pl.BlockSpec((1, d), lambda i: (0, 0))],
        out_specs=pl.BlockSpec((tm, d), lambda i: (i, 0)),
        out_shape=jax.ShapeDtypeStruct((s, d), F32),
        compiler_params=_params(("parallel",)),
        name="final_norm",
    )(x, g)


def _proj_rope_kernel(h_ref, w_ref, tab_ref, o_ref, *, bounds, scales, tn, parts):
    j = pl.program_id(1)
    b_qa, b_qb, b_qi, b_ka = bounds
    s_qa, s_qb, s_qi = scales
    is_a = (j < b_qa) | ((j >= b_qi) & (j < b_ka))
    scale = jnp.where(j < b_qa, s_qa, jnp.where(j < b_qb, s_qb, jnp.where(j < b_qi, s_qi, 1.0))).astype(F32)
    n = h_ref.shape[0] // parts
    w = w_ref[...]

    def rope128(acc, r):
        c = tab_ref[r, 0:LANES] * scale
        s = tab_ref[r, LANES:2 * LANES] * scale
        for g in range(tn // LANES):
            xg = acc[:, g * LANES:(g + 1) * LANES]
            o_ref[r, g * LANES:(g + 1) * LANES] = (xg * c + pltpu.roll(xg, 64, 1) * s).astype(o_ref.dtype)

    def rope64(acc, r):
        c = tab_ref[r, 2 * LANES:3 * LANES] * scale
        shi = tab_ref[r, 3 * LANES:4 * LANES] * scale
        slo = tab_ref[r, 4 * LANES:5 * LANES] * scale
        for g in range(tn // LANES):
            xg = acc[:, g * LANES:(g + 1) * LANES]
            o_ref[r, g * LANES:(g + 1) * LANES] = (
                xg * c + pltpu.roll(xg, 32, 1) * shi + pltpu.roll(xg, 96, 1) * slo).astype(o_ref.dtype)

    def body(epilogue):
        for p in range(parts):
            r = slice(p * n, (p + 1) * n)
            epilogue(jnp.dot(h_ref[r, :], w, preferred_element_type=F32), r)

    pl.when(is_a)(lambda: body(rope128))
    pl.when(jnp.logical_not(is_a))(lambda: body(rope64))


def _proj_rope(h, w_r, layer, tabs, seg):
    s, d = h.shape
    n = w_r.shape[2]
    tn = seg["tn"]
    tm = _pick(s, (1024, 512, 256, 128))
    kern = functools.partial(_proj_rope_kernel, bounds=seg["bounds"], scales=seg["scales"], tn=tn,
                             parts=4 if tm % 512 == 0 else 1)
    return pl.pallas_call(
        kern,
        grid=(s // tm, n // tn),
        in_specs=[
            pl.BlockSpec((tm, d), lambda i, j: (i, 0)),
            pl.BlockSpec((None, d, tn), lambda i, j: (layer, 0, j)),
            pl.BlockSpec((tm, 5 * LANES), lambda i, j: (i, 0)),
        ],
        out_specs=pl.BlockSpec((tm, tn), lambda i, j: (i, j)),
        out_shape=jax.ShapeDtypeStruct((s, n), BF16),
        compiler_params=_params(("parallel", "arbitrary")),
        name="proj_rope",
    )(h, w_r, tabs)


VA_PAD = 16
VA_ROWS = A_HEAD_DIM + VA_PAD


def _proj_v_kernel(h_ref, w_ref, va_ref, vb_ref, wt_ref, *, n_va, n_vb, kc):
    acc = jnp.dot(h_ref[...], w_ref[...], preferred_element_type=F32)
    acct = acc.T
    tm = acc.shape[0]
    ones_rows = jnp.where(lax.broadcasted_iota(I32, (VA_PAD, kc), 0) == 0, 1.0, 0.0).astype(va_ref.dtype)
    for c in range(tm // kc):
        cols = slice(c * kc, (c + 1) * kc)
        for g in range(n_va // A_HEAD_DIM):
            va_ref[c, g * VA_ROWS:g * VA_ROWS + A_HEAD_DIM, :] = (
                acct[g * A_HEAD_DIM:(g + 1) * A_HEAD_DIM, cols].astype(va_ref.dtype))
            va_ref[c, g * VA_ROWS + A_HEAD_DIM:(g + 1) * VA_ROWS, :] = ones_rows
        vb_ref[c] = acct[n_va:n_va + n_vb, cols].astype(vb_ref.dtype)
    wt_ref[...] = acct[n_va + n_vb:n_va + n_vb + LANES] * (IDX_HEADS ** -0.5)


def _proj_v(h, w_v, layer, n_va, n_vb, kc):
    s, d = h.shape
    nv = w_v.shape[2]
    tm = _pick(s, (1024, 512))
    tm = max(tm, kc)
    n_va_aug = n_va // A_HEAD_DIM * VA_ROWS
    kern = functools.partial(_proj_v_kernel, n_va=n_va, n_vb=n_vb, kc=kc)
    return pl.pallas_call(
        kern,
        grid=(s // tm,),
        in_specs=[pl.BlockSpec((tm, d), lambda i: (i, 0)),
                  pl.BlockSpec((None, d, nv), lambda i: (layer, 0, 0))],
        out_specs=[
            pl.BlockSpec((tm // kc, n_va_aug, kc), lambda i: (i, 0, 0)),
            pl.BlockSpec((tm // kc, n_vb, kc), lambda i: (i, 0, 0)),
            pl.BlockSpec((LANES, tm), lambda i: (0, i)),
        ],
        out_shape=[
            jax.ShapeDtypeStruct((s // kc, n_va_aug, kc), BF16),
            jax.ShapeDtypeStruct((s // kc, n_vb, kc), BF16),
            jax.ShapeDtypeStruct((LANES, s), F32),
        ],
        compiler_params=_params(("parallel",)),
        name="proj_v",
    )(h, w_v)


def _dsa_kernel(qa_ref, qi_ref, wt_ref, ka_ref, ke_ref, vt_ref, g_ref, o_ref,
                key_ref, hi_ref, lo_ref, acc_ref, m_ref, st_ref, p_ref, al_ref, *, seq, n_sel, hkv, kc):
    i = pl.program_id(0)
    per = kc // TQ
    nchunk = (i + per) // per
    t_idx = i * TQ + lax.broadcasted_iota(I32, (1, TQ), 1)
    row_iota = lax.broadcasted_iota(I32, (kc, TQ), 0)
    n_pairs = IDX_HEADS // 2
    pairs_per_dot = 2
    n_dots = n_pairs // pairs_per_dot

    def rows(c):
        return pl.ds(c * kc if isinstance(c, int) else pl.multiple_of(c * kc, kc), kc)

    qp = [jnp.concatenate([qi_ref[:, (b * pairs_per_dot + p) * LANES:(b * pairs_per_dot + p + 1) * LANES]
                           for p in range(pairs_per_dot)], axis=0) for b in range(n_dots)]
    wrows = [wt_ref[h:h + 1, :] for h in range(IDX_HEADS)]

    def score_chunk(c, _):
        ke = ke_ref[rows(c), :]
        k_even, k_odd = ke[:, :LANES], ke[:, LANES:]
        logits = lambda b: (_dot_nt(k_even, qp[b]), _dot_nt(k_odd, qp[b]))
        cur = logits(0)
        sc = jnp.zeros((kc, TQ), F32)
        for b in range(n_dots):
            nxt = logits(b + 1) if b + 1 < n_dots else None
            for p in range(pairs_per_dot):
                h = 2 * (b * pairs_per_dot + p)
                sc = sc + wrows[h] * jnp.maximum(cur[0][:, p * TQ:(p + 1) * TQ], 0.0)
                sc = sc + wrows[h + 1] * jnp.maximum(cur[1][:, p * TQ:(p + 1) * TQ], 0.0)
            cur = nxt
        sc = jnp.where(c * kc + row_iota <= t_idx, sc, NEG)
        b32 = lax.bitcast_convert_type(sc, I32)
        key = b32 ^ (lax.shift_right_arithmetic(b32, 31) & 0x7FFFFFFF)
        key_ref[rows(c), :] = key
        hi_ref[rows(c), :] = lax.shift_right_arithmetic(key, 16).astype(I16)
        lo_ref[rows(c), :] = ((key & 0xFFFF) - 32768).astype(I16)
        return 0

    lax.fori_loop(0, nchunk, score_chunk, 0)

    unroll = math.gcd(COUNT_UNROLL, seq // kc)
    ntrip = (nchunk + unroll - 1) // unroll
    span = unroll * kc

    def fill_neg(c, _):
        hi_ref[rows(c), :] = jnp.full((kc, TQ), NEG_HI, I16)
        lo_ref[rows(c), :] = jnp.full((kc, TQ), NEG_LO, I16)
        return 0

    lax.fori_loop(nchunk, ntrip * unroll, fill_neg, 0)
    n_virtual = (seq - ntrip * span).astype(F32)
    cslab = 64

    def count16(ref, cand, strict):
        def body(t, acc):
            r0 = pl.multiple_of(t * span, span)
            for u in range(unroll):
                v = ref[pl.ds(r0 + u * kc, kc), :]
                hit = jnp.where((v > cand) if strict else (v >= cand), jnp.int16(1), jnp.int16(0))
                acc = acc + _tree(jnp.add, [hit[r:r + cslab] for r in range(0, kc, cslab)])
            return acc
        part = lax.fori_loop(0, ntrip, body, jnp.zeros((cslab, TQ), I16))
        return _col_reduce(jnp.add, jnp.sum, part.astype(F32))

    def bisect16(count_fn, cnt_all):
        def step(it, carry):
            u, cnt_u = carry
            cand_u = u | lax.shift_left(jnp.int32(1), jnp.int32(15) - it)
            cnt = count_fn(cand_u - 32768)
            ok = cnt >= n_sel
            return jnp.where(ok, cand_u, u), jnp.where(ok, cnt, cnt_u)
        return lax.fori_loop(0, 16, step, (jnp.zeros((1, TQ), I32), cnt_all))

    def count_hi_ge(cand):
        return count16(hi_ref, cand.astype(I16), False) + jnp.where(cand <= NEG_HI, n_virtual, 0.0)

    u_hi, cnt_hi_ge = bisect16(count_hi_ge, jnp.full((1, TQ), float(seq), F32))
    t_hi = u_hi - 32768
    t_hi16 = t_hi.astype(I16)
    cnt_hi_gt = count16(hi_ref, t_hi16, True) + jnp.where(t_hi < NEG_HI, n_virtual, 0.0)

    def mask_lo(t, _):
        r0 = pl.multiple_of(t * span, span)
        for u in range(unroll):
            r = pl.ds(r0 + u * kc, kc)
            lo_ref[r, :] = jnp.where(hi_ref[r, :] == t_hi16, lo_ref[r, :], jnp.int16(I16_MIN))
        return 0

    lax.fori_loop(0, ntrip, mask_lo, 0)
    virt_lo = jnp.where(t_hi == NEG_HI, n_virtual, 0.0)

    def count_lo_ge(cand):
        return (cnt_hi_gt + count16(lo_ref, cand.astype(I16), False)
                + jnp.where(cand <= NEG_LO, virt_lo, 0.0))

    u_lo, cnt_thr = bisect16(count_lo_ge, cnt_hi_ge)
    t_lo = u_lo - 32768
    cnt_gt = cnt_hi_gt + count16(lo_ref, t_lo.astype(I16), True) + jnp.where(t_lo < NEG_LO, virt_lo, 0.0)
    thr = lax.shift_left(t_hi, 16) | u_lo
    n_tie = n_sel - cnt_gt
    excess_ties = jnp.max(cnt_thr) > n_sel

    m_ref[...] = jnp.full(m_ref.shape, M_INIT, F32)
    acc_ref[...] = jnp.zeros(acc_ref.shape, F32)
    eye = jnp.where(lax.broadcasted_iota(I32, (TQ, LANES), 0) == lax.broadcasted_iota(I32, (TQ, LANES), 1),
                    1.0, 0.0).astype(BF16)
    rhs = [jnp.concatenate(
        [jnp.concatenate([qa_ref[:, (A_GROUP * g + h) * A_HEAD_DIM:(A_GROUP * g + h + 1) * A_HEAD_DIM], eye],
                         axis=1) for h in range(A_GROUP)], axis=0) for g in range(hkv)]

    def qk(c, bias_b, g):
        kg = ka_ref[rows(c), g * A_HEAD_DIM:(g + 1) * A_HEAD_DIM]
        return _dot_nt(jnp.concatenate([kg, bias_b], axis=1), rhs[g])

    def attend_chunk(c, bias_b, st, next_first):
        for g in range(hkv):
            if g + 1 < hkv:
                st_next = qk(c, bias_b, g + 1)
            else:
                st_next = next_first() if next_first is not None else None
            m_old = m_ref[g]
            m_new = jnp.maximum(m_old, _col_reduce(jnp.maximum, jnp.max, st))
            alpha = jnp.exp2(m_old - m_new)
            p = jnp.exp2(st - m_new).astype(BF16)
            m_ref[g] = m_new
            vt = vt_ref[c, g * VA_ROWS:(g + 1) * VA_ROWS, :]
            acc_ref[g] = acc_ref[g] * alpha + jnp.dot(vt, p, preferred_element_type=F32)
            st = st_next
        return st

    def plain_bias(c):
        return jnp.where(key_ref[rows(c), :] >= thr, 0.0, NEG).astype(BF16)

    def pv_update(c, g, p, alpha):
        vt = vt_ref[c, g * VA_ROWS:(g + 1) * VA_ROWS, :]
        acc_ref[g] = acc_ref[g] * alpha + jnp.dot(vt, p, preferred_element_type=F32)

    def attend_plain(c, _):
        st, p_prev, alpha_prev = st_ref[...], p_ref[...], al_ref[...]
        c_next = jnp.minimum(c + 1, nchunk - 1)
        c_prev = jnp.maximum(c - 1, 0)
        bias_b = plain_bias(c)
        for g in range(hkv):
            st_next = qk(c, bias_b, g + 1) if g + 1 < hkv else qk(c_next, plain_bias(c_next), 0)
            if g == 0:
                pv_update(c_prev, hkv - 1, p_prev, alpha_prev)
            else:
                pv_update(c, g - 1, p_prev, alpha_prev)
            m_old = m_ref[g]
            m_new = jnp.maximum(m_old, _col_reduce(jnp.maximum, jnp.max, st))
            alpha_prev = jnp.exp2(m_old - m_new)
            p_prev = jnp.exp2(st - m_new).astype(BF16)
            m_ref[g] = m_new
            st = st_next
        st_ref[...] = st
        p_ref[...] = p_prev
        al_ref[...] = alpha_prev
        return 0

    def attend_ties(c, tie_seen):
        k = key_ref[rows(c), :]
        ltri = jnp.where(lax.broadcasted_iota(I32, (kc, kc), 0) > lax.broadcasted_iota(I32, (kc, kc), 1),
                         1.0, 0.0).astype(BF16)
        eq = jnp.where(k == thr, 1.0, 0.0)
        rank = jnp.dot(ltri, eq.astype(BF16), preferred_element_type=F32) + tie_seen
        keep = jnp.where(k > thr, 1.0, jnp.where(rank < n_tie, eq, 0.0))
        keep = jnp.where(c * kc + row_iota <= t_idx, keep, 0.0)
        bias_b = jnp.where(keep > 0.5, 0.0, NEG).astype(BF16)
        attend_chunk(c, bias_b, qk(c, bias_b, 0), None)
        return tie_seen + _col_reduce(jnp.add, jnp.sum, eq)

    def run_ties():
        lax.fori_loop(0, nchunk, attend_ties, jnp.zeros((1, TQ), F32))

    def run_plain():
        st_ref[...] = qk(0, plain_bias(0), 0)
        p_ref[...] = jnp.zeros(p_ref.shape, BF16)
        al_ref[...] = jnp.ones(al_ref.shape, F32)
        lax.fori_loop(0, nchunk, attend_plain, 0)
        pv_update(nchunk - 1, hkv - 1, p_ref[...], al_ref[...])

    lax.cond(excess_ties, run_ties, run_plain)

    ssq = jnp.zeros((1, TQ), F32)
    for g in range(hkv):
        o = acc_ref[g, 0:A_HEAD_DIM, :] / acc_ref[g, A_HEAD_DIM:A_HEAD_DIM + 1, :]
        acc_ref[g, 0:A_HEAD_DIM, :] = o
        sq = jnp.sum(o * o, axis=0, keepdims=True)
        for h in range(A_GROUP):
            ssq = ssq + sq[:, h * TQ:(h + 1) * TQ]
    rn = lax.rsqrt(ssq / (hkv * A_GROUP * A_HEAD_DIM) + EPS)
    for g in range(hkv):
        for h in range(A_GROUP):
            col = (A_GROUP * g + h) * A_HEAD_DIM
            oh = acc_ref[g, 0:A_HEAD_DIM, h * TQ:(h + 1) * TQ] * rn
            o_ref[:, col:col + A_HEAD_DIM] = (oh.T * g_ref[:, col:col + A_HEAD_DIM]).astype(o_ref.dtype)


def _dsa(pr, vat3, wt, g_a, lay, seq, n_sel):
    d_a = lay["d_a"]
    hkv = lay["hkv_a"]
    nka = hkv * A_HEAD_DIM
    kc = lay["kc"]
    assert TQ == LANES
    one = pl.Buffered(1)
    kern = functools.partial(_dsa_kernel, seq=seq, n_sel=float(n_sel), hkv=hkv, kc=kc)
    return pl.pallas_call(
        kern,
        grid=(seq // TQ,),
        in_specs=[
            pl.BlockSpec((TQ, d_a), lambda i: (i, 0)),
            pl.BlockSpec((TQ, IDX_HEADS * IDX_DIM), lambda i: (i, lay["off_qi"] // (IDX_HEADS * IDX_DIM))),
            pl.BlockSpec((LANES, TQ), lambda i: (0, i)),
            pl.BlockSpec((seq, nka), lambda i: (0, lay["off_ka"] // nka), pipeline_mode=one),
            pl.BlockSpec((seq, 2 * LANES), lambda i: (0, lay["off_misc"] // (2 * LANES)), pipeline_mode=one),
            pl.BlockSpec((seq // kc, hkv * VA_ROWS, kc), lambda i: (0, 0, 0), pipeline_mode=one),
            pl.BlockSpec((1, d_a), lambda i: (0, 0)),
        ],
        out_specs=pl.BlockSpec((TQ, d_a), lambda i: (i, 0)),
        out_shape=jax.ShapeDtypeStruct((seq, d_a), BF16),
        scratch_shapes=[
            pltpu.VMEM((seq, TQ), I32),
            pltpu.VMEM((seq, TQ), I16),
            pltpu.VMEM((seq, TQ), I16),
            pltpu.VMEM((hkv, VA_ROWS, A_GROUP * TQ), F32),
            pltpu.VMEM((hkv, 1, A_GROUP * TQ), F32),
            pltpu.VMEM((kc, A_GROUP * TQ), F32),
            pltpu.VMEM((kc, A_GROUP * TQ), BF16),
            pltpu.VMEM((1, A_GROUP * TQ), F32),
        ],
        compiler_params=_params(("arbitrary",)),
        name="dsa",
    )(pr, pr, wt, pr, pr, vat3, g_a)


def _swa_kernel(qb_ref, kp_ref, kc_ref, vp_ref, vc_ref, se_ref, so_ref, g_ref, o_ref, ot_ref, *, hkv):
    i = pl.program_id(0)
    npair = B_GROUP // 2
    kwin = jnp.concatenate([kp_ref[...], kc_ref[...]], axis=0).astype(F32)
    r = lax.broadcasted_iota(I32, (2 * WINDOW, TQ), 0) - WINDOW
    q = lax.broadcasted_iota(I32, (2 * WINDOW, TQ), 1)
    ok = (r <= q) & (q - r < WINDOW) & (i * WINDOW + r >= 0)
    bias_b = jnp.where(ok, 0.0, NEG).astype(BF16)
    eye = jnp.where(lax.broadcasted_iota(I32, (TQ, LANES), 0) == lax.broadcasted_iota(I32, (TQ, LANES), 1),
                    1.0, 0.0).astype(BF16)
    ones_rows = jnp.where(lax.broadcasted_iota(I32, (VA_PAD, 2 * WINDOW), 0) == 0, 1.0, 0.0).astype(BF16)
    lane = lax.broadcasted_iota(I32, (2 * WINDOW, LANES), 1)
    nq = npair * TQ
    ssq = jnp.zeros((1, TQ), F32)
    for g in range(hkv):
        kcol = kwin[:, (g // 2) * LANES:(g // 2 + 1) * LANES]
        rolled = pltpu.roll(kcol, 64, 1)
        lo_src, hi_src = (kcol, rolled) if g % 2 == 0 else (rolled, kcol)
        k_even = jnp.where(lane < 64, lo_src, 0.0).astype(BF16)
        k_odd = jnp.where(lane >= 64, hi_src, 0.0).astype(BF16)
        lhs = jnp.concatenate([jnp.concatenate([k_even, bias_b], axis=1),
                               jnp.concatenate([k_odd, bias_b], axis=1)], axis=0)
        rhs = jnp.concatenate(
            [jnp.concatenate([qb_ref[:, (npair * g + p) * LANES:(npair * g + p + 1) * LANES], eye], axis=1)
             for p in range(npair)], axis=0)
        st = _dot_nt(lhs, rhs)
        ps, ms = [], []
        for half, s_ref in ((0, se_ref), (1, so_ref)):
            sth = st[half * 2 * WINDOW:(half + 1) * 2 * WINDOW]
            m = jnp.maximum(_col_reduce(jnp.maximum, jnp.max, sth), s_ref[g])
            ps.append(jnp.exp2(sth - m).astype(BF16))
            ms.append(m)
        v_aug = jnp.concatenate(
            [jnp.concatenate([vp_ref[g * B_HEAD_DIM:(g + 1) * B_HEAD_DIM, :],
                              vc_ref[g * B_HEAD_DIM:(g + 1) * B_HEAD_DIM, :]], axis=1), ones_rows], axis=0)
        ot = jnp.dot(v_aug, jnp.concatenate(ps, axis=1), preferred_element_type=F32)
        outs = []
        for half, s_ref in ((0, se_ref), (1, so_ref)):
            den = ot[B_HEAD_DIM:B_HEAD_DIM + 1, half * nq:(half + 1) * nq] + jnp.exp2(s_ref[g] - ms[half])
            outs.append(ot[0:B_HEAD_DIM, half * nq:(half + 1) * nq] / den)
        for p in range(npair):
            blk = jnp.concatenate([outs[0][:, p * TQ:(p + 1) * TQ], outs[1][:, p * TQ:(p + 1) * TQ]],
                                  axis=0)
            ot_ref[npair * g + p] = blk
            ssq = ssq + jnp.sum(blk * blk, axis=0, keepdims=True)
    rn = lax.rsqrt(ssq / (hkv * B_GROUP * B_HEAD_DIM) + EPS)
    for c in range(hkv * npair):
        o_ref[:, c * LANES:(c + 1) * LANES] = (
            (ot_ref[c] * rn).T * g_ref[:, c * LANES:(c + 1) * LANES]).astype(o_ref.dtype)


def _swa(pr, vbt3, sink_e, sink_o, g_b, lay, seq):
    d_b = lay["d_b"]
    hkv = lay["hkv_b"]
    kc = lay["kc"]
    per = kc // TQ
    nvb = hkv * B_HEAD_DIM
    off_kb = lay["off_misc"] + 2 * LANES
    assert nvb % LANES == 0 and off_kb % nvb == 0
    col_kb = off_kb // nvb
    prev = lambda i: jnp.maximum(i - 1, 0)
    kern = functools.partial(_swa_kernel, hkv=hkv)
    return pl.pallas_call(
        kern,
        grid=(seq // TQ,),
        in_specs=[
            pl.BlockSpec((TQ, d_b), lambda i: (i, lay["off_qb"] // d_b)),
            pl.BlockSpec((TQ, nvb), lambda i: (prev(i), col_kb)),
            pl.BlockSpec((TQ, nvb), lambda i: (i, col_kb)),
            pl.BlockSpec((None, nvb, TQ), lambda i: (prev(i) // per, 0, prev(i) % per)),
            pl.BlockSpec((None, nvb, TQ), lambda i: (i // per, 0, i % per)),
            pl.BlockSpec((hkv, 1, (B_GROUP // 2) * TQ), lambda i: (0, 0, 0)),
            pl.BlockSpec((hkv, 1, (B_GROUP // 2) * TQ), lambda i: (0, 0, 0)),
            pl.BlockSpec((1, d_b), lambda i: (0, 0)),
        ],
        out_specs=pl.BlockSpec((TQ, d_b), lambda i: (i, 0)),
        out_shape=jax.ShapeDtypeStruct((seq, d_b), BF16),
        scratch_shapes=[pltpu.VMEM((hkv * B_GROUP // 2, LANES, TQ), F32)],
        compiler_params=_params(("parallel",)),
        name="swa",
    )(pr, pr, pr, vbt3, vbt3, sink_e, sink_o, g_b)


def _oproj_kernel(na_ref, nb_ref, wa_ref, wb_ref, x_ref, gt_ref, o_ref):
    acc = jnp.dot(na_ref[...], wa_ref[...], preferred_element_type=F32)
    acc = acc + jnp.dot(nb_ref[...], wb_ref[...], preferred_element_type=F32)
    o_ref[...] = x_ref[...] + gt_ref[...] * acc


def _oproj(na, nb, w_out, layer, x, gate):
    s, d_a = na.shape
    d_b = nb.shape[1]
    d = x.shape[1]
    assert d_a == d_b
    tm = _pick(s, (1024, 512, 256, 128))
    tn = _pick(d, (512, 256, 128))
    return pl.pallas_call(
        _oproj_kernel,
        grid=(s // tm, d // tn),
        in_specs=[
            pl.BlockSpec((tm, d_a), lambda i, j: (i, 0)),
            pl.BlockSpec((tm, d_b), lambda i, j: (i, 0)),
            pl.BlockSpec((None, d_a, tn), lambda i, j: (layer, 0, j)),
            pl.BlockSpec((None, d_b, tn), lambda i, j: (layer, 1, j)),
            pl.BlockSpec((tm, tn), lambda i, j: (i, j)),
            pl.BlockSpec((1, tn), lambda i, j: (0, j)),
        ],
        out_specs=pl.BlockSpec((tm, tn), lambda i, j: (i, j)),
        out_shape=jax.ShapeDtypeStruct((s, d), F32),
        compiler_params=_params(("parallel", "arbitrary")),
        name="oproj",
    )(na, nb, w_out, w_out, x, gate)


HALO = 8
UP_PARTS = 1
FF_ALIGN = 512


def _up_kernel(h_ref, wg_ref, wv_ref, cwg_ref, cwv_ref, cbg_ref, cbv_ref, o_ref, eg_ref, ev_ref, *, tm, parts):
    i = pl.program_id(1)

    @pl.when(i == 0)
    def _():
        eg_ref[0:HALO, :] = jnp.zeros((HALO, eg_ref.shape[1]), F32)
        ev_ref[0:HALO, :] = jnp.zeros((HALO, ev_ref.shape[1]), F32)

    @pl.when(i > 0)
    def _():
        eg_ref[0:HALO, :] = eg_ref[tm:tm + HALO, :]
        ev_ref[0:HALO, :] = ev_ref[tm:tm + HALO, :]

    def conv(e_ref, cw_ref, cb_ref, r0, n):
        y = cb_ref[...] + cw_ref[CONV_WIDTH - 1:CONV_WIDTH, :] * e_ref[HALO + r0:HALO + r0 + n, :]
        for k in range(1, CONV_WIDTH):
            y = y + cw_ref[CONV_WIDTH - 1 - k:CONV_WIDTH - k, :] * e_ref[HALO + r0 - k:HALO + r0 - k + n, :]
        return y

    n = tm // parts
    for p in range(parts):
        h = h_ref[p * n:(p + 1) * n, :]
        eg_ref[HALO + p * n:HALO + (p + 1) * n, :] = jnp.dot(h, wg_ref[...], preferred_element_type=F32)
        ev_ref[HALO + p * n:HALO + (p + 1) * n, :] = jnp.dot(h, wv_ref[...], preferred_element_type=F32)
    for p in range(parts):
        gte = conv(eg_ref, cwg_ref, cbg_ref, p * n, n)
        val = conv(ev_ref, cwv_ref, cbv_ref, p * n, n)
        o_ref[p * n:(p + 1) * n, :] = (gte * (1.0 / (1.0 + jnp.exp(-gte))) * val).astype(o_ref.dtype)


def _up(h, w_up, layer, conv_w, conv_b):
    s, d = h.shape
    f = w_up.shape[2] // 2
    tn = _pick(f, (512, 256, 128))
    tm = _pick(s, (1024, 512, 256, 128))
    nf = f // tn
    kern = functools.partial(_up_kernel, tm=tm, parts=UP_PARTS if tm % (UP_PARTS * 128) == 0 else 1)
    gate = lambda j, i: (layer, 0, j)
    val = lambda j, i: (layer, 0, nf + j)
    return pl.pallas_call(
        kern,
        grid=(nf, s // tm),
        in_specs=[
            pl.BlockSpec((tm, d), lambda j, i: (i, 0)),
            pl.BlockSpec((None, d, tn), gate), pl.BlockSpec((None, d, tn), val),
            pl.BlockSpec((None, CONV_WIDTH, tn), gate), pl.BlockSpec((None, CONV_WIDTH, tn), val),
            pl.BlockSpec((None, 1, tn), gate), pl.BlockSpec((None, 1, tn), val),
        ],
        out_specs=pl.BlockSpec((tm, tn), lambda j, i: (i, j)),
        out_shape=jax.ShapeDtypeStruct((s, f), BF16),
        scratch_shapes=[pltpu.VMEM((tm + HALO, tn), F32), pltpu.VMEM((tm + HALO, tn), F32)],
        compiler_params=_params(("arbitrary", "arbitrary")),
        name="up_conv_gate",
    )(h, w_up, w_up, conv_w, conv_w, conv_b, conv_b)


def _down_kernel(a_ref, w_ref, x_ref, gt_ref, o_ref):
    o_ref[...] = x_ref[...] + gt_ref[...] * jnp.dot(a_ref[...], w_ref[...], preferred_element_type=F32)


def _down(act, wd, layer, x, gate):
    s, f = act.shape
    d = x.shape[1]
    tm = _pick(s, (512, 256, 128))
    tn = _pick(d, (512, 256, 128))
    return pl.pallas_call(
        _down_kernel,
        grid=(s // tm, d // tn),
        in_specs=[
            pl.BlockSpec((tm, f), lambda i, j: (i, 0)),
            pl.BlockSpec((None, f, tn), lambda i, j: (layer, 0, j)),
            pl.BlockSpec((tm, tn), lambda i, j: (i, j)),
            pl.BlockSpec((1, tn), lambda i, j: (0, j)),
        ],
        out_specs=pl.BlockSpec((tm, tn), lambda i, j: (i, j)),
        out_shape=jax.ShapeDtypeStruct((s, d), F32),
        compiler_params=_params(("parallel", "arbitrary")),
        name="down",
    )(act, wd, x, gate)


def _cast_pad_kernel(x_ref, o_ref, *, axis, n_real, n_out):
    r = pl.program_id(axis) % n_out

    @pl.when(r < n_real)
    def _():
        o_ref[...] = x_ref[...].astype(o_ref.dtype)

    @pl.when(r >= n_real)
    def _():
        o_ref[...] = jnp.zeros(o_ref.shape, o_ref.dtype)


def _pad_unit(f, padf):
    g = np.gcd(f, padf) if padf else f
    return _pick(int(g), (512, 256, 128))


def _cast_up(w_up, f, padf):
    depth, d, _ = w_up.shape
    cw = _pad_unit(f, padf)
    n_real, n_out = f // cw, (f + padf) // cw
    src = lambda l, t: (l, 0, (t // n_out) * n_real + jnp.minimum(t % n_out, n_real - 1))
    kern = functools.partial(_cast_pad_kernel, axis=1, n_real=n_real, n_out=n_out)
    return pl.pallas_call(
        kern,
        grid=(depth, 2 * n_out),
        in_specs=[pl.BlockSpec((None, d, cw), src)],
        out_specs=pl.BlockSpec((None, d, cw), lambda l, t: (l, 0, t)),
        out_shape=jax.ShapeDtypeStruct((depth, d, 2 * (f + padf)), BF16),
        compiler_params=_params(("parallel", "arbitrary")),
        name="cast_up",
    )(w_up)


def _cast_down(w_down, padf):
    depth, f, d = w_down.shape
    rt = _pad_unit(f, padf)
    n_real, n_out = f // rt, (f + padf) // rt
    kern = functools.partial(_cast_pad_kernel, axis=1, n_real=n_real, n_out=n_out)
    return pl.pallas_call(
        kern,
        grid=(depth, n_out),
        in_specs=[pl.BlockSpec((None, rt, d), lambda l, t: (l, jnp.minimum(t, n_real - 1), 0))],
        out_specs=pl.BlockSpec((None, rt, d), lambda l, t: (l, t, 0)),
        out_shape=jax.ShapeDtypeStruct((depth, f + padf, d), BF16),
        compiler_params=_params(("parallel", "arbitrary")),
        name="cast_down",
    )(w_down)


def _layout(d_model, seq):
    d_a = d_model // 2
    d_b = d_model - d_a
    ha = d_a // A_HEAD_DIM
    hkv_a = ha // A_GROUP
    hb = d_b // B_HEAD_DIM
    hkv_b = hb // B_GROUP
    n_qi = IDX_HEADS * IDX_DIM
    n_ka = hkv_a * A_HEAD_DIM
    n_kb = hkv_b * B_HEAD_DIM
    misc_raw = 2 * LANES + n_kb
    seg_w = [d_a, d_b, n_qi, n_ka]
    tn = 512
    while any(w % tn for w in seg_w) or tn > misc_raw + LANES:
        tn //= 2
    misc_w = -(-misc_raw // tn) * tn
    off_qb = d_a
    off_qi = off_qb + d_b
    off_ka = off_qi + n_qi
    off_misc = off_ka + n_ka
    n_r = off_misc + misc_w
    bounds = (off_qb // tn, off_qi // tn, off_ka // tn, off_misc // tn)
    scales = (A_HEAD_DIM ** -0.5 * LOG2E, B_HEAD_DIM ** -0.5 * LOG2E, IDX_DIM ** -0.5)
    kc = min(KC, seq)
    assert d_a == d_b and seq % kc == 0 and kc % TQ == 0 and off_qi % n_qi == 0 and off_ka % n_ka == 0
    assert off_misc % (2 * LANES) == 0
    return dict(d_a=d_a, d_b=d_b, ha=ha, hkv_a=hkv_a, hb=hb, hkv_b=hkv_b, n_qi=n_qi, n_ka=n_ka, n_kb=n_kb,
                tn=tn, misc_w=misc_w, off_qb=off_qb, off_qi=off_qi, off_ka=off_ka, off_misc=off_misc,
                n_r=n_r, bounds=bounds, scales=scales, kc=kc, n_va=n_ka, n_vb=n_kb)


M_COPY, M_SHIFT, M_KI_EVEN, M_KI_ODD, M_ZERO, M_WI = range(6)
SHIFT = (IDX_DIM + IDX_HEADS) % LANES


def _relayout_kernel(a_idx, b_idx, mode, a_ref, b_ref, o_ref):
    m = mode[pl.program_id(1)]
    lane = lax.broadcasted_iota(I32, a_ref.shape, 1)

    def put(v):
        o_ref[...] = v.astype(o_ref.dtype)

    pl.when(m == M_COPY)(lambda: put(a_ref[...]))
    pl.when(m == M_SHIFT)(lambda: put(jnp.where(lane < LANES - SHIFT, pltpu.roll(a_ref[...], LANES - SHIFT, 1),
                                                pltpu.roll(b_ref[...], LANES - SHIFT, 1))))
    pl.when(m == M_KI_EVEN)(lambda: put(jnp.where(lane < IDX_DIM, a_ref[...], 0.0)))
    pl.when(m == M_KI_ODD)(lambda: put(jnp.where(lane >= IDX_DIM, pltpu.roll(a_ref[...], IDX_DIM, 1), 0.0)))
    pl.when(m == M_ZERO)(lambda: put(jnp.zeros(a_ref.shape, F32)))
    pl.when(m == M_WI)(lambda: put(jnp.where(lane < IDX_HEADS, pltpu.roll(a_ref[...], LANES - IDX_DIM, 1), 0.0)))


def _relayout(w_in, tiles):
    depth, d, n_src = w_in.shape
    a_idx, b_idx, modes = [], [], []
    for mode, col in tiles:
        if mode == M_SHIFT:
            assert col % LANES == SHIFT
        elif mode != M_ZERO:
            assert col % LANES == 0
        a = col // LANES
        a_idx.append(a)
        b_idx.append(a + 1 if mode == M_SHIFT else (b_idx[-1] if b_idx else 0))
        modes.append(mode)
    assert max(b_idx) * LANES < n_src
    as_i32 = lambda v: jnp.asarray(np.asarray(v, np.int32))
    n_tiles = len(tiles)
    return pl.pallas_call(
        _relayout_kernel,
        grid_spec=pltpu.PrefetchScalarGridSpec(
            num_scalar_prefetch=3,
            grid=(depth, n_tiles),
            in_specs=[pl.BlockSpec((None, d, LANES), lambda l, t, a, b, m: (l, 0, a[t])),
                      pl.BlockSpec((None, d, LANES), lambda l, t, a, b, m: (l, 0, b[t]))],
            out_specs=pl.BlockSpec((None, d, LANES), lambda l, t, a, b, m: (l, 0, t)),
        ),
        out_shape=jax.ShapeDtypeStruct((depth, d, n_tiles * LANES), BF16),
        compiler_params=_params(("parallel", "arbitrary")),
        name="relayout_w_in",
    )(as_i32(a_idx), as_i32(b_idx), as_i32(modes), w_in, w_in)


def _prep_w_in(w_in, lay):
    d_a, d_b = lay["d_a"], lay["d_b"]
    widths = (d_a, lay["n_ka"], lay["n_ka"], lay["n_qi"], IDX_DIM, IDX_HEADS, d_b, lay["n_kb"], lay["n_kb"])
    o_qa, o_ka, o_va, o_qi, o_ki, o_wi, o_qb, o_kb, o_vb = [int(v) for v in np.cumsum((0,) + widths[:-1])]
    assert o_wi == o_ki + IDX_DIM and 2 * IDX_DIM == LANES
    seg = lambda mode, off, width: [(mode, off + k * LANES) for k in range(width // LANES)]
    pad_m = lay["misc_w"] - (2 * LANES + lay["n_kb"])
    w_r = _relayout(w_in, seg(M_COPY, o_qa, d_a) + seg(M_SHIFT, o_qb, d_b) + seg(M_COPY, o_qi, lay["n_qi"])
                    + seg(M_COPY, o_ka, lay["n_ka"]) + [(M_KI_EVEN, o_ki), (M_KI_ODD, o_ki)]
                    + seg(M_SHIFT, o_kb, lay["n_kb"]) + [(M_ZERO, 0)] * (pad_m // LANES))
    w_v = _relayout(w_in, seg(M_COPY, o_va, lay["n_va"]) + seg(M_SHIFT, o_vb, lay["n_vb"]) + [(M_WI, o_ki)])
    return w_r, w_v


def _rope_tabs(positions):
    pos = positions[0].astype(F32)
    inv_a = ROPE_THETA ** (-jnp.arange(0, A_HEAD_DIM, 2, dtype=F32) / A_HEAD_DIM)
    ang_a = pos[:, None] * inv_a
    ca, sa = jnp.cos(ang_a), jnp.sin(ang_a)
    inv_6 = ROPE_THETA ** (-jnp.arange(0, B_HEAD_DIM, 2, dtype=F32) / B_HEAD_DIM)
    ang_6 = pos[:, None] * inv_6
    c6, s6 = jnp.cos(ang_6), jnp.sin(ang_6)
    z6 = jnp.zeros_like(s6)
    return jnp.concatenate([
        ca, ca, -sa, sa,
        c6, c6, c6, c6,
        z6, s6, z6, s6,
        -s6, z6, -s6, z6], axis=-1)


def kernel(x, c, positions, w_ada, b_ada, g_mix, w_in, g_out_a, g_out_b, sinks, w_out, g_ffn, w_up,
           conv_w, conv_b, w_down, g_final):
    b, seq, d = x.shape
    assert b == 1 and IDX_DIM == B_HEAD_DIM
    depth = w_ada.shape[0]
    lay = _layout(d, seq)
    n_sel = min(TOPK_MAX, seq // 4)
    hkv_b = lay["hkv_b"]

    w_r, w_v = _prep_w_in(w_in, lay)
    w_out_b = w_out.astype(BF16)
    f = w_down.shape[1]
    padf = -f % FF_ALIGN
    halves = lambda a: jnp.concatenate(
        [a[..., :f], jnp.zeros(a.shape[:-1] + (padf,), a.dtype), a[..., f:],
         jnp.zeros(a.shape[:-1] + (padf,), a.dtype)], axis=-1)
    w_up_b = _cast_up(w_up, f, padf)
    w_down_b = _cast_down(w_down, padf)
    conv_wp = halves(conv_w)
    conv_b3 = halves(conv_b)[:, None, :]
    tabs = _rope_tabs(positions)
    sk = (sinks * LOG2E).reshape(depth, hkv_b, B_GROUP // 2, 2)
    rep = lambda a: jnp.repeat(a, TQ, axis=-1).reshape(depth, hkv_b, 1, (B_GROUP // 2) * TQ)
    sink_e, sink_o = rep(sk[..., 0]), rep(sk[..., 1])

    mod = _ada(c.reshape(d, 1), w_ada, b_ada)
    xs = x[0]
    for l in range(depth):
        sh_m, sc_m, gt_m, sh_f, sc_f, gt_f = [mod[l, :, n * d:(n + 1) * d] for n in range(N_MOD)]
        h = _norm_mod(xs, g_mix[l][None, :], sc_m, sh_m)
        pr = _proj_rope(h, w_r, l, tabs, lay)
        vat3, vbt3, wt = _proj_v(h, w_v, l, lay["n_va"], lay["n_vb"], lay["kc"])
        na = _dsa(pr, vat3, wt, g_out_a[l][None, :], lay, seq, n_sel)
        nb = _swa(pr, vbt3, sink_e[l], sink_o[l], g_out_b[l][None, :], lay, seq)
        xs = _oproj(na, nb, w_out_b, l, xs, gt_m)
        h = _norm_mod(xs, g_ffn[l][None, :], sc_f, sh_f)
        act = _up(h, w_up_b, l, conv_wp, conv_b3)
        xs = _down(act, w_down_b, l, xs, gt_f)
    return _final_norm(xs, g_final[None, :])[None]
```

```python
import functools
import math

import numpy as np
import jax
import jax.numpy as jnp
from jax import lax
from jax.experimental import pallas as pl
from jax.experimental.pallas import tpu as pltpu

A_HEAD_DIM = 128
A_GROUP = 4
IDX_HEADS = 16
IDX_DIM = 64
TOPK_MAX = 256
B_HEAD_DIM = 64
B_GROUP = 8
WINDOW = 128
CONV_WIDTH = 3
ROPE_THETA = 10000.0
EPS = 1e-6
NEG = -1e30
M_INIT = -1e29
N_MOD = 6
LOG2E = 1.4426950408889634

LANES = 128
VMEM_LIMIT = 56 * 1024 * 1024

TQ = 128
KC = 512
COUNT_UNROLL = 2

F32 = jnp.float32
BF16 = jnp.bfloat16
I32 = jnp.int32
I16 = jnp.int16

_NEG_BITS = int(np.float32(NEG).view(np.int32))
KEY_NEG = _NEG_BITS ^ ((_NEG_BITS >> 31) & 0x7FFFFFFF)
NEG_HI = KEY_NEG >> 16
NEG_LO = (KEY_NEG & 0xFFFF) - 32768
I16_MIN = -32768


def _params(sem, vmem=VMEM_LIMIT):
    return pltpu.CompilerParams(dimension_semantics=sem, vmem_limit_bytes=vmem)


def _pick(n, prefs):
    for p in prefs:
        if n % p == 0:
            return p
    return n


def _dot_nt(a, b):
    return lax.dot_general(a, b, (((1,), (1,)), ((), ())), preferred_element_type=F32)


def _tree(op, parts):
    parts = list(parts)
    while len(parts) > 1:
        nxt = [op(parts[a], parts[a + 1]) for a in range(0, len(parts) - 1, 2)]
        if len(parts) % 2:
            nxt.append(parts[-1])
        parts = nxt
    return parts[0]


def _col_reduce(op, red, x):
    slabs = [x[r:r + 8] for r in range(0, x.shape[0], 8)]
    return red(_tree(op, slabs), axis=0, keepdims=True)


def _ada_kernel(c_ref, w_ref, b_ref, o_ref):
    k = pl.program_id(2)

    @pl.when(k == 0)
    def _():
        o_ref[...] = b_ref[...]

    c = c_ref[...]
    ca = c * (1.0 / (1.0 + jnp.exp(-c)))
    o_ref[...] += jnp.sum(w_ref[...] * ca, axis=0, keepdims=True)


def _ada(c_col, w_ada, b_ada):
    depth, d, n = w_ada.shape
    tk = _pick(d, (2048, 1024, 512, 256, 128))
    tn = _pick(n, (2048, 1024, 512, 256, 128))
    return pl.pallas_call(
        _ada_kernel,
        grid=(depth, n // tn, d // tk),
        in_specs=[
            pl.BlockSpec((tk, 1), lambda l, j, k: (k, 0)),
            pl.BlockSpec((None, tk, tn), lambda l, j, k: (l, k, j)),
            pl.BlockSpec((None, 1, tn), lambda l, j, k: (l, 0, j)),
        ],
        out_specs=pl.BlockSpec((None, 1, tn), lambda l, j, k: (l, 0, j)),
        out_shape=jax.ShapeDtypeStruct((depth, 1, n), F32),
        compiler_params=_params(("parallel", "parallel", "arbitrary")),
        name="ada",
    )(c_col, w_ada, b_ada.reshape(depth, 1, n))


def _norm_mod_kernel(x_ref, g_ref, sc_ref, sh_ref, o_ref):
    x = x_ref[...]
    ms = jnp.mean(x * x, axis=-1, keepdims=True)
    y = x * lax.rsqrt(ms + EPS) * g_ref[...]
    o_ref[...] = (y * (1.0 + sc_ref[...]) + sh_ref[...]).astype(o_ref.dtype)


def _norm_kernel(x_ref, g_ref, o_ref):
    x = x_ref[...]
    ms = jnp.mean(x * x, axis=-1, keepdims=True)
    o_ref[...] = (x * lax.rsqrt(ms + EPS) * g_ref[...]).astype(o_ref.dtype)


def _norm_mod(x, g, sc, sh):
    s, d = x.shape
    tm = _pick(s, (512, 256, 128))
    row = pl.BlockSpec((1, d), lambda i: (0, 0))
    return pl.pallas_call(
        _norm_mod_kernel,
        grid=(s // tm,),
        in_specs=[pl.BlockSpec((tm, d), lambda i: (i, 0)), row, row, row],
        out_specs=pl.BlockSpec((tm, d), lambda i: (i, 0)),
        out_shape=jax.ShapeDtypeStruct((s, d), BF16),
        compiler_params=_params(("parallel",)),
        name="norm_mod",
    )(x, g, sc, sh)


def _final_norm(x, g):
    s, d = x.shape
    tm = _pick(s, (512, 256, 128))
    return pl.pallas_call(
        _norm_kernel,
        grid=(s // tm,),
        in_specs=[pl.BlockSpec((tm, d), lambda i: (i, 0)), pl.BlockSpec((1, d), lambda i: (0, 0))],
        out_specs=pl.BlockSpec((tm, d), lambda i: (i, 0)),
        out_shape=jax.ShapeDtypeStruct((s, d), F32),
        compiler_params=_params(("parallel",)),
        name="final_norm",
    )(x, g)


def _proj_rope_kernel(h_ref, w_ref, tab_ref, o_ref, *, bounds, scales, tn, parts):
    j = pl.program_id(1)
    b_qa, b_qb, b_qi, b_ka = bounds
    s_qa, s_qb, s_qi = scales
    is_a = (j < b_qa) | ((j >= b_qi) & (j < b_ka))
    scale = jnp.where(j < b_qa, s_qa, jnp.where(j < b_qb, s_qb, jnp.where(j < b_qi, s_qi, 1.0))).astype(F32)
    n = h_ref.shape[0] // parts
    w = w_ref[...]

    def rope128(acc, r):
        c = tab_ref[r, 0:LANES] * scale
        s = tab_ref[r, LANES:2 * LANES] * scale
        for g in range(tn // LANES):
            xg = acc[:, g * LANES:(g + 1) * LANES]
            o_ref[r, g * LANES:(g + 1) * LANES] = (xg * c + pltpu.roll(xg, 64, 1) * s).astype(o_ref.dtype)

    def rope64(acc, r):
        c = tab_ref[r, 2 * LANES:3 * LANES] * scale
        shi = tab_ref[r, 3 * LANES:4 * LANES] * scale
        slo = tab_ref[r, 4 * LANES:5 * LANES] * scale
        for g in range(tn // LANES):
            xg = acc[:, g * LANES:(g + 1) * LANES]
            o_ref[r, g * LANES:(g + 1) * LANES] = (
                xg * c + pltpu.roll(xg, 32, 1) * shi + pltpu.roll(xg, 96, 1) * slo).astype(o_ref.dtype)

    def body(epilogue):
        for p in range(parts):
            r = slice(p * n, (p + 1) * n)
            epilogue(jnp.dot(h_ref[r, :], w, preferred_element_type=F32), r)

    pl.when(is_a)(lambda: body(rope128))
    pl.when(jnp.logical_not(is_a))(lambda: body(rope64))


def _proj_rope(h, w_r, layer, tabs, seg):
    s, d = h.shape
    n = w_r.shape[2]
    tn = seg["tn"]
    tm = _pick(s, (1024, 512, 256, 128))
    kern = functools.partial(_proj_rope_kernel, bounds=seg["bounds"], scales=seg["scales"], tn=tn,
                             parts=4 if tm % 512 == 0 else 1)
    return pl.pallas_call(
        kern,
        grid=(s // tm, n // tn),
        in_specs=[
            pl.BlockSpec((tm, d), lambda i, j: (i, 0)),
            pl.BlockSpec((None, d, tn), lambda i, j: (layer, 0, j)),
            pl.BlockSpec((tm, 5 * LANES), lambda i, j: (i, 0)),
        ],
        out_specs=pl.BlockSpec((tm, tn), lambda i, j: (i, j)),
        out_shape=jax.ShapeDtypeStruct((s, n), BF16),
        compiler_params=_params(("parallel", "arbitrary")),
        name="proj_rope",
    )(h, w_r, tabs)


VA_PAD = 16
VA_ROWS = A_HEAD_DIM + VA_PAD


def _proj_v_kernel(h_ref, w_ref, va_ref, vb_ref, wt_ref, *, n_va, n_vb, kc):
    acc = jnp.dot(h_ref[...], w_ref[...], preferred_element_type=F32)
    acct = acc.T
    tm = acc.shape[0]
    ones_rows = jnp.where(lax.broadcasted_iota(I32, (VA_PAD, kc), 0) == 0, 1.0, 0.0).astype(va_ref.dtype)
    for c in range(tm // kc):
        cols = slice(c * kc, (c + 1) * kc)
        for g in range(n_va // A_HEAD_DIM):
            va_ref[c, g * VA_ROWS:g * VA_ROWS + A_HEAD_DIM, :] = (
                acct[g * A_HEAD_DIM:(g + 1) * A_HEAD_DIM, cols].astype(va_ref.dtype))
            va_ref[c, g * VA_ROWS + A_HEAD_DIM:(g + 1) * VA_ROWS, :] = ones_rows
        vb_ref[c] = acct[n_va:n_va + n_vb, cols].astype(vb_ref.dtype)
    wt_ref[...] = acct[n_va + n_vb:n_va + n_vb + LANES] * (IDX_HEADS ** -0.5)


def _proj_v(h, w_v, layer, n_va, n_vb, kc):
    s, d = h.shape
    nv = w_v.shape[2]
    tm = _pick(s, (1024, 512))
    tm = max(tm, kc)
    n_va_aug = n_va // A_HEAD_DIM * VA_ROWS
    kern = functools.partial(_proj_v_kernel, n_va=n_va, n_vb=n_vb, kc=kc)
    return pl.pallas_call(
        kern,
        grid=(s // tm,),
        in_specs=[pl.BlockSpec((tm, d), lambda i: (i, 0)),
                  pl.BlockSpec((None, d, nv), lambda i: (layer, 0, 0))],
        out_specs=[
            pl.BlockSpec((tm // kc, n_va_aug, kc), lambda i: (i, 0, 0)),
            pl.BlockSpec((tm // kc, n_vb, kc), lambda i: (i, 0, 0)),
            pl.BlockSpec((LANES, tm), lambda i: (0, i)),
        ],
        out_shape=[
            jax.ShapeDtypeStruct((s // kc, n_va_aug, kc), BF16),
            jax.ShapeDtypeStruct((s // kc, n_vb, kc), BF16),
            jax.ShapeDtypeStruct((LANES, s), F32),
        ],
        compiler_params=_params(("parallel",)),
        name="proj_v",
    )(h, w_v)


def _dsa_kernel(qa_ref, qi_ref, wt_ref, ka_ref, ke_ref, vt_ref, g_ref, o_ref,
                key_ref, hi_ref, lo_ref, acc_ref, m_ref, st_ref, p_ref, al_ref, *, seq, n_sel, hkv, kc):
    i = pl.program_id(0)
    per = kc // TQ
    nchunk = (i + per) // per
    t_idx = i * TQ + lax.broadcasted_iota(I32, (1, TQ), 1)
    row_iota = lax.broadcasted_iota(I32, (kc, TQ), 0)
    n_pairs = IDX_HEADS // 2
    pairs_per_dot = 2
    n_dots = n_pairs // pairs_per_dot

    def rows(c):
        return pl.ds(c * kc if isinstance(c, int) else pl.multiple_of(c * kc, kc), kc)

    qp = [jnp.concatenate([qi_ref[:, (b * pairs_per_dot + p) * LANES:(b * pairs_per_dot + p + 1) * LANES]
                           for p in range(pairs_per_dot)], axis=0) for b in range(n_dots)]
    wrows = [wt_ref[h:h + 1, :] for h in range(IDX_HEADS)]

    def score_chunk(c, _):
        ke = ke_ref[rows(c), :]
        k_even, k_odd = ke[:, :LANES], ke[:, LANES:]
        logits = lambda b: (_dot_nt(k_even, qp[b]), _dot_nt(k_odd, qp[b]))
        cur = logits(0)
        sc = jnp.zeros((kc, TQ), F32)
        for b in range(n_dots):
            nxt = logits(b + 1) if b + 1 < n_dots else None
            for p in range(pairs_per_dot):
                h = 2 * (b * pairs_per_dot + p)
                sc = sc + wrows[h] * jnp.maximum(cur[0][:, p * TQ:(p + 1) * TQ], 0.0)
                sc = sc + wrows[h + 1] * jnp.maximum(cur[1][:, p * TQ:(p + 1) * TQ], 0.0)
            cur = nxt
        sc = jnp.where(c * kc + row_iota <= t_idx, sc, NEG)
        b32 = lax.bitcast_convert_type(sc, I32)
        key = b32 ^ (lax.shift_right_arithmetic(b32, 31) & 0x7FFFFFFF)
        key_ref[rows(c), :] = key
        hi_ref[rows(c), :] = lax.shift_right_arithmetic(key, 16).astype(I16)
        lo_ref[rows(c), :] = ((key & 0xFFFF) - 32768).astype(I16)
        return 0

    lax.fori_loop(0, nchunk, score_chunk, 0)

    unroll = math.gcd(COUNT_UNROLL, seq // kc)
    ntrip = (nchunk + unroll - 1) // unroll
    span = unroll * kc

    def fill_neg(c, _):
        hi_ref[rows(c), :] = jnp.full((kc, TQ), NEG_HI, I16)
        lo_ref[rows(c), :] = jnp.full((kc, TQ), NEG_LO, I16)
        return 0

    lax.fori_loop(nchunk, ntrip * unroll, fill_neg, 0)
    n_virtual = (seq - ntrip * span).astype(F32)
    cslab = 64

    def count16(ref, cand, strict):
        def body(t, acc):
            r0 = pl.multiple_of(t * span, span)
            for u in range(unroll):
                v = ref[pl.ds(r0 + u * kc, kc), :]
                hit = jnp.where((v > cand) if strict else (v >= cand), jnp.int16(1), jnp.int16(0))
                acc = acc + _tree(jnp.add, [hit[r:r + cslab] for r in range(0, kc, cslab)])
            return acc
        part = lax.fori_loop(0, ntrip, body, jnp.zeros((cslab, TQ), I16))
        return _col_reduce(jnp.add, jnp.sum, part.astype(F32))

    def bisect16(count_fn, cnt_all):
        def step(it, carry):
            u, cnt_u = carry
            cand_u = u | lax.shift_left(jnp.int32(1), jnp.int32(15) - it)
            cnt = count_fn(cand_u - 32768)
            ok = cnt >= n_sel
            return jnp.where(ok, cand_u, u), jnp.where(ok, cnt, cnt_u)
        return lax.fori_loop(0, 16, step, (jnp.zeros((1, TQ), I32), cnt_all))

    def count_hi_ge(cand):
        return count16(hi_ref, cand.astype(I16), False) + jnp.where(cand <= NEG_HI, n_virtual, 0.0)

    u_hi, cnt_hi_ge = bisect16(count_hi_ge, jnp.full((1, TQ), float(seq), F32))
    t_hi = u_hi - 32768
    t_hi16 = t_hi.astype(I16)
    cnt_hi_gt = count16(hi_ref, t_hi16, True) + jnp.where(t_hi < NEG_HI, n_virtual, 0.0)

    def mask_lo(t, _):
        r0 = pl.multiple_of(t * span, span)
        for u in range(unroll):
            r = pl.ds(r0 + u * kc, kc)
            lo_ref[r, :] = jnp.where(hi_ref[r, :] == t_hi16, lo_ref[r, :], jnp.int16(I16_MIN))
        return 0

    lax.fori_loop(0, ntrip, mask_lo, 0)
    virt_lo = jnp.where(t_hi == NEG_HI, n_virtual, 0.0)

    def count_lo_ge(cand):
        return (cnt_hi_gt + count16(lo_ref, cand.astype(I16), False)
                + jnp.where(cand <= NEG_LO, virt_lo, 0.0))

    u_lo, cnt_thr = bisect16(count_lo_ge, cnt_hi_ge)
    t_lo = u_lo - 32768
    cnt_gt = cnt_hi_gt + count16(lo_ref, t_lo.astype(I16), True) + jnp.where(t_lo < NEG_LO, virt_lo, 0.0)
    thr = lax.shift_left(t_hi, 16) | u_lo
    n_tie = n_sel - cnt_gt
    excess_ties = jnp.max(cnt_thr) > n_sel

    m_ref[...] = jnp.full(m_ref.shape, M_INIT, F32)
    acc_ref[...] = jnp.zeros(acc_ref.shape, F32)
    eye = jnp.where(lax.broadcasted_iota(I32, (TQ, LANES), 0) == lax.broadcasted_iota(I32, (TQ, LANES), 1),
                    1.0, 0.0).astype(BF16)
    rhs = [jnp.concatenate(
        [jnp.concatenate([qa_ref[:, (A_GROUP * g + h) * A_HEAD_DIM:(A_GROUP * g + h + 1) * A_HEAD_DIM], eye],
                         axis=1) for h in range(A_GROUP)], axis=0) for g in range(hkv)]

    def qk(c, bias_b, g):
        kg = ka_ref[rows(c), g * A_HEAD_DIM:(g + 1) * A_HEAD_DIM]
        return _dot_nt(jnp.concatenate([kg, bias_b], axis=1), rhs[g])

    def attend_chunk(c, bias_b, st, next_first):
        for g in range(hkv):
            if g + 1 < hkv:
                st_next = qk(c, bias_b, g + 1)
            else:
                st_next = next_first() if next_first is not None else None
            m_old = m_ref[g]
            m_new = jnp.maximum(m_old, _col_reduce(jnp.maximum, jnp.max, st))
            alpha = jnp.exp2(m_old - m_new)
            p = jnp.exp2(st - m_new).astype(BF16)
            m_ref[g] = m_new
            vt = vt_ref[c, g * VA_ROWS:(g + 1) * VA_ROWS, :]
            acc_ref[g] = acc_ref[g] * alpha + jnp.dot(vt, p, preferred_element_type=F32)
            st = st_next
        return st

    def plain_bias(c):
        return jnp.where(key_ref[rows(c), :] >= thr, 0.0, NEG).astype(BF16)

    def pv_update(c, g, p, alpha):
        vt = vt_ref[c, g * VA_ROWS:(g + 1) * VA_ROWS, :]
        acc_ref[g] = acc_ref[g] * alpha + jnp.dot(vt, p, preferred_element_type=F32)

    def attend_plain(c, _):
        st, p_prev, alpha_prev = st_ref[...], p_ref[...], al_ref[...]
        c_next = jnp.minimum(c + 1, nchunk - 1)
        c_prev = jnp.maximum(c - 1, 0)
        bias_b = plain_bias(c)
        for g in range(hkv):
            st_next = qk(c, bias_b, g + 1) if g + 1 < hkv else qk(c_next, plain_bias(c_next), 0)
            if g == 0:
                pv_update(c_prev, hkv - 1, p_prev, alpha_prev)
            else:
                pv_update(c, g - 1, p_prev, alpha_prev)
            m_old = m_ref[g]
            m_new = jnp.maximum(m_old, _col_reduce(jnp.maximum, jnp.max, st))
            alpha_prev = jnp.exp2(m_old - m_new)
            p_prev = jnp.exp2(st - m_new).astype(BF16)
            m_ref[g] = m_new
            st = st_next
        st_ref[...] = st
        p_ref[...] = p_prev
        al_ref[...] = alpha_prev
        return 0

    def attend_ties(c, tie_seen):
        k = key_ref[rows(c), :]
        ltri = jnp.where(lax.broadcasted_iota(I32, (kc, kc), 0) > lax.broadcasted_iota(I32, (kc, kc), 1),
                         1.0, 0.0).astype(BF16)
        eq = jnp.where(k == thr, 1.0, 0.0)
        rank = jnp.dot(ltri, eq.astype(BF16), preferred_element_type=F32) + tie_seen
        keep = jnp.where(k > thr, 1.0, jnp.where(rank < n_tie, eq, 0.0))
        keep = jnp.where(c * kc + row_iota <= t_idx, keep, 0.0)
        bias_b = jnp.where(keep > 0.5, 0.0, NEG).astype(BF16)
        attend_chunk(c, bias_b, qk(c, bias_b, 0), None)
        return tie_seen + _col_reduce(jnp.add, jnp.sum, eq)

    def run_ties():
        lax.fori_loop(0, nchunk, attend_ties, jnp.zeros((1, TQ), F32))

    def run_plain():
        st_ref[...] = qk(0, plain_bias(0), 0)
        p_ref[...] = jnp.zeros(p_ref.shape, BF16)
        al_ref[...] = jnp.ones(al_ref.shape, F32)
        lax.fori_loop(0, nchunk, attend_plain, 0)
        pv_update(nchunk - 1, hkv - 1, p_ref[...], al_ref[...])

    lax.cond(excess_ties, run_ties, run_plain)

    ssq = jnp.zeros((1, TQ), F32)
    for g in range(hkv):
        o = acc_ref[g, 0:A_HEAD_DIM, :] / acc_ref[g, A_HEAD_DIM:A_HEAD_DIM + 1, :]
        acc_ref[g, 0:A_HEAD_DIM, :] = o
        sq = jnp.sum(o * o, axis=0, keepdims=True)
        for h in range(A_GROUP):
            ssq = ssq + sq[:, h * TQ:(h + 1) * TQ]
    rn = lax.rsqrt(ssq / (hkv * A_GROUP * A_HEAD_DIM) + EPS)
    for g in range(hkv):
        for h in range(A_GROUP):
            col = (A_GROUP * g + h) * A_HEAD_DIM
            oh = acc_ref[g, 0:A_HEAD_DIM, h * TQ:(h + 1) * TQ] * rn
            o_ref[:, col:col + A_HEAD_DIM] = (oh.T * g_ref[:, col:col + A_HEAD_DIM]).astype(o_ref.dtype)


def _dsa(pr, vat3, wt, g_a, lay, seq, n_sel):
    d_a = lay["d_a"]
    hkv = lay["hkv_a"]
    nka = hkv * A_HEAD_DIM
    kc = lay["kc"]
    assert TQ == LANES
    one = pl.Buffered(1)
    kern = functools.partial(_dsa_kernel, seq=seq, n_sel=float(n_sel), hkv=hkv, kc=kc)
    return pl.pallas_call(
        kern,
        grid=(seq // TQ,),
        in_specs=[
            pl.BlockSpec((TQ, d_a), lambda i: (i, 0)),
            pl.BlockSpec((TQ, IDX_HEADS * IDX_DIM), lambda i: (i, lay["off_qi"] // (IDX_HEADS * IDX_DIM))),
            pl.BlockSpec((LANES, TQ), lambda i: (0, i)),
            pl.BlockSpec((seq, nka), lambda i: (0, lay["off_ka"] // nka), pipeline_mode=one),
            pl.BlockSpec((seq, 2 * LANES), lambda i: (0, lay["off_misc"] // (2 * LANES)), pipeline_mode=one),
            pl.BlockSpec((seq // kc, hkv * VA_ROWS, kc), lambda i: (0, 0, 0), pipeline_mode=one),
            pl.BlockSpec((1, d_a), lambda i: (0, 0)),
        ],
        out_specs=pl.BlockSpec((TQ, d_a), lambda i: (i, 0)),
        out_shape=jax.ShapeDtypeStruct((seq, d_a), BF16),
        scratch_shapes=[
            pltpu.VMEM((seq, TQ), I32),
            pltpu.VMEM((seq, TQ), I16),
            pltpu.VMEM((seq, TQ), I16),
            pltpu.VMEM((hkv, VA_ROWS, A_GROUP * TQ), F32),
            pltpu.VMEM((hkv, 1, A_GROUP * TQ), F32),
            pltpu.VMEM((kc, A_GROUP * TQ), F32),
            pltpu.VMEM((kc, A_GROUP * TQ), BF16),
            pltpu.VMEM((1, A_GROUP * TQ), F32),
        ],
        compiler_params=_params(("arbitrary",)),
        name="dsa",
    )(pr, pr, wt, pr, pr, vat3, g_a)


def _swa_kernel(qb_ref, kp_ref, kc_ref, vp_ref, vc_ref, se_ref, so_ref, g_ref, o_ref, ot_ref, *, hkv):
    i = pl.program_id(0)
    npair = B_GROUP // 2
    kwin = jnp.concatenate([kp_ref[...], kc_ref[...]], axis=0).astype(F32)
    r = lax.broadcasted_iota(I32, (2 * WINDOW, TQ), 0) - WINDOW
    q = lax.broadcasted_iota(I32, (2 * WINDOW, TQ), 1)
    ok = (r <= q) & (q - r < WINDOW) & (i * WINDOW + r >= 0)
    bias_b = jnp.where(ok, 0.0, NEG).astype(BF16)
    eye = jnp.where(lax.broadcasted_iota(I32, (TQ, LANES), 0) == lax.broadcasted_iota(I32, (TQ, LANES), 1),
                    1.0, 0.0).astype(BF16)
    ones_rows = jnp.where(lax.broadcasted_iota(I32, (VA_PAD, 2 * WINDOW), 0) == 0, 1.0, 0.0).astype(BF16)
    lane = lax.broadcasted_iota(I32, (2 * WINDOW, LANES), 1)
    nq = npair * TQ
    ssq = jnp.zeros((1, TQ), F32)
    for g in range(hkv):
        kcol = kwin[:, (g // 2) * LANES:(g // 2 + 1) * LANES]
        rolled = pltpu.roll(kcol, 64, 1)
        lo_src, hi_src = (kcol, rolled) if g % 2 == 0 else (rolled, kcol)
        k_even = jnp.where(lane < 64, lo_src, 0.0).astype(BF16)
        k_odd = jnp.where(lane >= 64, hi_src, 0.0).astype(BF16)
        lhs = jnp.concatenate([jnp.concatenate([k_even, bias_b], axis=1),
                               jnp.concatenate([k_odd, bias_b], axis=1)], axis=0)
        rhs = jnp.concatenate(
            [jnp.concatenate([qb_ref[:, (npair * g + p) * LANES:(npair * g + p + 1) * LANES], eye], axis=1)
             for p in range(npair)], axis=0)
        st = _dot_nt(lhs, rhs)
        ps, ms = [], []
        for half, s_ref in ((0, se_ref), (1, so_ref)):
            sth = st[half * 2 * WINDOW:(half + 1) * 2 * WINDOW]
            m = jnp.maximum(_col_reduce(jnp.maximum, jnp.max, sth), s_ref[g])
            ps.append(jnp.exp2(sth - m).astype(BF16))
            ms.append(m)
        v_aug = jnp.concatenate(
            [jnp.concatenate([vp_ref[g * B_HEAD_DIM:(g + 1) * B_HEAD_DIM, :],
                              vc_ref[g * B_HEAD_DIM:(g + 1) * B_HEAD_DIM, :]], axis=1), ones_rows], axis=0)
        ot = jnp.dot(v_aug, jnp.concatenate(ps, axis=1), preferred_element_type=F32)
        outs = []
        for half, s_ref in ((0, se_ref), (1, so_ref)):
            den = ot[B_HEAD_DIM:B_HEAD_DIM + 1, half * nq:(half + 1) * nq] + jnp.exp2(s_ref[g] - ms[half])
            outs.append(ot[0:B_HEAD_DIM, half * nq:(half + 1) * nq] / den)
        for p in range(npair):
            blk = jnp.concatenate([outs[0][:, p * TQ:(p + 1) * TQ], outs[1][:, p * TQ:(p + 1) * TQ]],
                                  axis=0)
            ot_ref[npair * g + p] = blk
            ssq = ssq + jnp.sum(blk * blk, axis=0, keepdims=True)
    rn = lax.rsqrt(ssq / (hkv * B_GROUP * B_HEAD_DIM) + EPS)
    for c in range(hkv * npair):
        o_ref[:, c * LANES:(c + 1) * LANES] = (
            (ot_ref[c] * rn).T * g_ref[:, c * LANES:(c + 1) * LANES]).astype(o_ref.dtype)


def _swa(pr, vbt3, sink_e, sink_o, g_b, lay, seq):
    d_b = lay["d_b"]
    hkv = lay["hkv_b"]
    kc = lay["kc"]
    per = kc // TQ
    nvb = hkv * B_HEAD_DIM
    off_kb = lay["off_misc"] + 2 * LANES
    assert nvb % LANES == 0 and off_kb % nvb == 0
    col_kb = off_kb // nvb
    prev = lambda i: jnp.maximum(i - 1, 0)
    kern = functools.partial(_swa_kernel, hkv=hkv)
    return pl.pallas_call(
        kern,
        grid=(seq // TQ,),
        in_specs=[
            pl.BlockSpec((TQ, d_b), lambda i: (i, lay["off_qb"] // d_b)),
            pl.BlockSpec((TQ, nvb), lambda i: (prev(i), col_kb)),
            pl.BlockSpec((TQ, nvb), lambda i: (i, col_kb)),
            pl.BlockSpec((None, nvb, TQ), lambda i: (prev(i) // per, 0, prev(i) % per)),
            pl.BlockSpec((None, nvb, TQ), lambda i: (i // per, 0, i % per)),
            pl.BlockSpec((hkv, 1, (B_GROUP // 2) * TQ), lambda i: (0, 0, 0)),
            pl.BlockSpec((hkv, 1, (B_GROUP // 2) * TQ), lambda i: (0, 0, 0)),
            pl.BlockSpec((1, d_b), lambda i: (0, 0)),
        ],
        out_specs=pl.BlockSpec((TQ, d_b), lambda i: (i, 0)),
        out_shape=jax.ShapeDtypeStruct((seq, d_b), BF16),
        scratch_shapes=[pltpu.VMEM((hkv * B_GROUP // 2, LANES, TQ), F32)],
        compiler_params=_params(("parallel",)),
        name="swa",
    )(pr, pr, pr, vbt3, vbt3, sink_e, sink_o, g_b)


def _oproj_kernel(na_ref, nb_ref, wa_ref, wb_ref, x_ref, gt_ref, o_ref):
    acc = jnp.dot(na_ref[...], wa_ref[...], preferred_element_type=F32)
    acc = acc + jnp.dot(nb_ref[...], wb_ref[...], preferred_element_type=F32)
    o_ref[...] = x_ref[...] + gt_ref[...] * acc


def _oproj(na, nb, w_out, layer, x, gate):
    s, d_a = na.shape
    d_b = nb.shape[1]
    d = x.shape[1]
    assert d_a == d_b
    tm = _pick(s, (1024, 512, 256, 128))
    tn = _pick(d, (512, 256, 128))
    return pl.pallas_call(
        _oproj_kernel,
        grid=(s // tm, d // tn),
        in_specs=[
            pl.BlockSpec((tm, d_a), lambda i, j: (i, 0)),
            pl.BlockSpec((tm, d_b), lambda i, j: (i, 0)),
            pl.BlockSpec((None, d_a, tn), lambda i, j: (layer, 0, j)),
            pl.BlockSpec((None, d_b, tn), lambda i, j: (layer, 1, j)),
            pl.BlockSpec((tm, tn), lambda i, j: (i, j)),
            pl.BlockSpec((1, tn), lambda i, j: (0, j)),
        ],
        out_specs=pl.BlockSpec((tm, tn), lambda i, j: (i, j)),
        out_shape=jax.ShapeDtypeStruct((s, d), F32),
        compiler_params=_params(("parallel", "arbitrary")),
        name="oproj",
    )(na, nb, w_out, w_out, x, gate)


HALO = 8
UP_PARTS = 1
FF_ALIGN = 512


def _up_kernel(h_ref, wg_ref, wv_ref, cwg_ref, cwv_ref, cbg_ref, cbv_ref, o_ref, eg_ref, ev_ref, *, tm, parts):
    i = pl.program_id(1)

    @pl.when(i == 0)
    def _():
        eg_ref[0:HALO, :] = jnp.zeros((HALO, eg_ref.shape[1]), F32)
        ev_ref[0:HALO, :] = jnp.zeros((HALO, ev_ref.shape[1]), F32)

    @pl.when(i > 0)
    def _():
        eg_ref[0:HALO, :] = eg_ref[tm:tm + HALO, :]
        ev_ref[0:HALO, :] = ev_ref[tm:tm + HALO, :]

    def conv(e_ref, cw_ref, cb_ref, r0, n):
        y = cb_ref[...] + cw_ref[CONV_WIDTH - 1:CONV_WIDTH, :] * e_ref[HALO + r0:HALO + r0 + n, :]
        for k in range(1, CONV_WIDTH):
            y = y + cw_ref[CONV_WIDTH - 1 - k:CONV_WIDTH - k, :] * e_ref[HALO + r0 - k:HALO + r0 - k + n, :]
        return y

    n = tm // parts
    for p in range(parts):
        h = h_ref[p * n:(p + 1) * n, :]
        eg_ref[HALO + p * n:HALO + (p + 1) * n, :] = jnp.dot(h, wg_ref[...], preferred_element_type=F32)
        ev_ref[HALO + p * n:HALO + (p + 1) * n, :] = jnp.dot(h, wv_ref[...], preferred_element_type=F32)
    for p in range(parts):
        gte = conv(eg_ref, cwg_ref, cbg_ref, p * n, n)
        val = conv(ev_ref, cwv_ref, cbv_ref, p * n, n)
        o_ref[p * n:(p + 1) * n, :] = (gte * (1.0 / (1.0 + jnp.exp(-gte))) * val).astype(o_ref.dtype)


def _up(h, w_up, layer, conv_w, conv_b):
    s, d = h.shape
    f = w_up.shape[2] // 2
    tn = _pick(f, (512, 256, 128))
    tm = _pick(s, (1024, 512, 256, 128))
    nf = f // tn
    kern = functools.partial(_up_kernel, tm=tm, parts=UP_PARTS if tm % (UP_PARTS * 128) == 0 else 1)
    gate = lambda j, i: (layer, 0, j)
    val = lambda j, i: (layer, 0, nf + j)
    return pl.pallas_call(
        kern,
        grid=(nf, s // tm),
        in_specs=[
            pl.BlockSpec((tm, d), lambda j, i: (i, 0)),
            pl.BlockSpec((None, d, tn), gate), pl.BlockSpec((None, d, tn), val),
            pl.BlockSpec((None, CONV_WIDTH, tn), gate), pl.BlockSpec((None, CONV_WIDTH, tn), val),
            pl.BlockSpec((None, 1, tn), gate), pl.BlockSpec((None, 1, tn), val),
        ],
        out_specs=pl.BlockSpec((tm, tn), lambda j, i: (i, j)),
        out_shape=jax.ShapeDtypeStruct((s, f), BF16),
        scratch_shapes=[pltpu.VMEM((tm + HALO, tn), F32), pltpu.VMEM((tm + HALO, tn), F32)],
        compiler_params=_params(("arbitrary", "arbitrary")),
        name="up_conv_gate",
    )(h, w_up, w_up, conv_w, conv_w, conv_b, conv_b)


def _down_kernel(a_ref, w_ref, x_ref, gt_ref, o_ref):
    o_ref[...] = x_ref[...] + gt_ref[...] * jnp.dot(a_ref[...], w_ref[...], preferred_element_type=F32)


def _down(act, wd, layer, x, gate):
    s, f = act.shape
    d = x.shape[1]
    tm = _pick(s, (512, 256, 128))
    tn = _pick(d, (512, 256, 128))
    return pl.pallas_call(
        _down_kernel,
        grid=(s // tm, d // tn),
        in_specs=[
            pl.BlockSpec((tm, f), lambda i, j: (i, 0)),
            pl.BlockSpec((None, f, tn), lambda i, j: (layer, 0, j)),
            pl.BlockSpec((tm, tn), lambda i, j: (i, j)),
            pl.BlockSpec((1, tn), lambda i, j: (0, j)),
        ],
        out_specs=pl.BlockSpec((tm, tn), lambda i, j: (i, j)),
        out_shape=jax.ShapeDtypeStruct((s, d), F32),
        compiler_params=_params(("parallel", "arbitrary")),
        name="down",
    )(act, wd, x, gate)


def _cast_pad_kernel(x_ref, o_ref, *, axis, n_real, n_out):
    r = pl.program_id(axis) % n_out

    @pl.when(r < n_real)
    def _():
        o_ref[...] = x_ref[...].astype(o_ref.dtype)

    @pl.when(r >= n_real)
    def _():
        o_ref[...] = jnp.zeros(o_ref.shape, o_ref.dtype)


def _pad_unit(f, padf):
    g = np.gcd(f, padf) if padf else f
    return _pick(int(g), (512, 256, 128))


def _cast_up(w_up, f, padf):
    depth, d, _ = w_up.shape
    cw = _pad_unit(f, padf)
    n_real, n_out = f // cw, (f + padf) // cw
    src = lambda l, t: (l, 0, (t // n_out) * n_real + jnp.minimum(t % n_out, n_real - 1))
    kern = functools.partial(_cast_pad_kernel, axis=1, n_real=n_real, n_out=n_out)
    return pl.pallas_call(
        kern,
        grid=(depth, 2 * n_out),
        in_specs=[pl.BlockSpec((None, d, cw), src)],
        out_specs=pl.BlockSpec((None, d, cw), lambda l, t: (l, 0, t)),
        out_shape=jax.ShapeDtypeStruct((depth, d, 2 * (f + padf)), BF16),
        compiler_params=_params(("parallel", "arbitrary")),
        name="cast_up",
    )(w_up)


def _cast_down(w_down, padf):
    depth, f, d = w_down.shape
    rt = _pad_unit(f, padf)
    n_real, n_out = f // rt, (f + padf) // rt
    kern = functools.partial(_cast_pad_kernel, axis=1, n_real=n_real, n_out=n_out)
    return pl.pallas_call(
        kern,
        grid=(depth, n_out),
        in_specs=[pl.BlockSpec((None, rt, d), lambda l, t: (l, jnp.minimum(t, n_real - 1), 0))],
        out_specs=pl.BlockSpec((None, rt, d), lambda l, t: (l, t, 0)),
        out_shape=jax.ShapeDtypeStruct((depth, f + padf, d), BF16),
        compiler_params=_params(("parallel", "arbitrary")),
        name="cast_down",
    )(w_down)


def _layout(d_model, seq):
    d_a = d_model // 2
    d_b = d_model - d_a
    ha = d_a // A_HEAD_DIM
    hkv_a = ha // A_GROUP
    hb = d_b // B_HEAD_DIM
    hkv_b = hb // B_GROUP
    n_qi = IDX_HEADS * IDX_DIM
    n_ka = hkv_a * A_HEAD_DIM
    n_kb = hkv_b * B_HEAD_DIM
    misc_raw = 2 * LANES + n_kb
    seg_w = [d_a, d_b, n_qi, n_ka]
    tn = 512
    while any(w % tn for w in seg_w) or tn > misc_raw + LANES:
        tn //= 2
    misc_w = -(-misc_raw // tn) * tn
    off_qb = d_a
    off_qi = off_qb + d_b
    off_ka = off_qi + n_qi
    off_misc = off_ka + n_ka
    n_r = off_misc + misc_w
    bounds = (off_qb // tn, off_qi // tn, off_ka // tn, off_misc // tn)
    scales = (A_HEAD_DIM ** -0.5 * LOG2E, B_HEAD_DIM ** -0.5 * LOG2E, IDX_DIM ** -0.5)
    kc = min(KC, seq)
    assert d_a == d_b and seq % kc == 0 and kc % TQ == 0 and off_qi % n_qi == 0 and off_ka % n_ka == 0
    assert off_misc % (2 * LANES) == 0
    return dict(d_a=d_a, d_b=d_b, ha=ha, hkv_a=hkv_a, hb=hb, hkv_b=hkv_b, n_qi=n_qi, n_ka=n_ka, n_kb=n_kb,
                tn=tn, misc_w=misc_w, off_qb=off_qb, off_qi=off_qi, off_ka=off_ka, off_misc=off_misc,
                n_r=n_r, bounds=bounds, scales=scales, kc=kc, n_va=n_ka, n_vb=n_kb)


M_COPY, M_KI_EVEN, M_KI_ODD, M_ZERO, M_WI = range(5)
ROW_ALIGN = 16


def _relayout_kernel(off, mode, a_ref, o_ref):
    m = mode[pl.program_id(1)]
    lane = lax.broadcasted_iota(I32, o_ref.shape, 1)

    def put(fn):
        def _():
            o_ref[...] = fn(a_ref[0].T).astype(o_ref.dtype)
        return _

    pl.when(m == M_COPY)(put(lambda x: x))
    pl.when(m == M_KI_EVEN)(put(lambda x: jnp.where(lane < IDX_DIM, x, 0.0)))
    pl.when(m == M_KI_ODD)(put(lambda x: jnp.where(lane >= IDX_DIM, pltpu.roll(x, IDX_DIM, 1), 0.0)))
    pl.when(m == M_ZERO)(put(lambda x: jnp.zeros_like(x)))
    pl.when(m == M_WI)(put(lambda x: jnp.where(lane < IDX_HEADS, pltpu.roll(x, LANES - IDX_DIM, 1), 0.0)))


def _relayout(w_in_t, tiles):
    depth, n_src, d = w_in_t.shape
    offs = [col for _, col in tiles]
    assert all(o % ROW_ALIGN == 0 and o + LANES <= n_src for o in offs)
    as_i32 = lambda v: jnp.asarray(np.asarray(v, np.int32))
    n_tiles = len(tiles)
    window = pl.BlockSpec((pl.Element(1), pl.Element(LANES), pl.Element(d)),
                          lambda l, t, off, m: (l, pl.multiple_of(off[t], ROW_ALIGN), 0))
    return pl.pallas_call(
        _relayout_kernel,
        grid_spec=pltpu.PrefetchScalarGridSpec(
            num_scalar_prefetch=2,
            grid=(depth, n_tiles),
            in_specs=[window],
            out_specs=pl.BlockSpec((None, d, LANES), lambda l, t, off, m: (l, 0, t)),
        ),
        out_shape=jax.ShapeDtypeStruct((depth, d, n_tiles * LANES), BF16),
        compiler_params=_params(("parallel", "arbitrary")),
        name="relayout_w_in",
    )(as_i32(offs), as_i32([m for m, _ in tiles]), w_in_t)


def _prep_w_in(w_in, lay):
    d_a, d_b = lay["d_a"], lay["d_b"]
    widths = (d_a, lay["n_ka"], lay["n_ka"], lay["n_qi"], IDX_DIM, IDX_HEADS, d_b, lay["n_kb"], lay["n_kb"])
    o_qa, o_ka, o_va, o_qi, o_ki, o_wi, o_qb, o_kb, o_vb = [int(v) for v in np.cumsum((0,) + widths[:-1])]
    assert o_wi == o_ki + IDX_DIM and 2 * IDX_DIM == LANES
    seg = lambda off, width: [(M_COPY, off + k * LANES) for k in range(width // LANES)]
    pad_m = lay["misc_w"] - (2 * LANES + lay["n_kb"])
    w_in_t = jnp.swapaxes(w_in, 1, 2)
    w_r = _relayout(w_in_t, seg(o_qa, d_a) + seg(o_qb, d_b) + seg(o_qi, lay["n_qi"]) + seg(o_ka, lay["n_ka"])
                    + [(M_KI_EVEN, o_ki), (M_KI_ODD, o_ki)] + seg(o_kb, lay["n_kb"])
                    + [(M_ZERO, 0)] * (pad_m // LANES))
    w_v = _relayout(w_in_t, seg(o_va, lay["n_va"]) + seg(o_vb, lay["n_vb"]) + [(M_WI, o_ki)])
    return w_r, w_v


def _rope_tabs(positions):
    pos = positions[0].astype(F32)
    inv_a = ROPE_THETA ** (-jnp.arange(0, A_HEAD_DIM, 2, dtype=F32) / A_HEAD_DIM)
    ang_a = pos[:, None] * inv_a
    ca, sa = jnp.cos(ang_a), jnp.sin(ang_a)
    inv_6 = ROPE_THETA ** (-jnp.arange(0, B_HEAD_DIM, 2, dtype=F32) / B_HEAD_DIM)
    ang_6 = pos[:, None] * inv_6
    c6, s6 = jnp.cos(ang_6), jnp.sin(ang_6)
    z6 = jnp.zeros_like(s6)
    return jnp.concatenate([
        ca, ca, -sa, sa,
        c6, c6, c6, c6,
        z6, s6, z6, s6,
        -s6, z6, -s6, z6], axis=-1)


def kernel(x, c, positions, w_ada, b_ada, g_mix, w_in, g_out_a, g_out_b, sinks, w_out, g_ffn, w_up,
           conv_w, conv_b, w_down, g_final):
    b, seq, d = x.shape
    assert b == 1 and IDX_DIM == B_HEAD_DIM
    depth = w_ada.shape[0]
    lay = _layout(d, seq)
    n_sel = min(TOPK_MAX, seq // 4)
    hkv_b = lay["hkv_b"]

    w_r, w_v = _prep_w_in(w_in, lay)
    w_out_b = w_out.astype(BF16)
    f = w_down.shape[1]
    padf = -f % FF_ALIGN
    halves = lambda a: jnp.concatenate(
        [a[..., :f], jnp.zeros(a.shape[:-1] + (padf,), a.dtype), a[..., f:],
         jnp.zeros(a.shape[:-1] + (padf,), a.dtype)], axis=-1)
    w_up_b = _cast_up(w_up, f, padf)
    w_down_b = _cast_down(w_down, padf)
    conv_wp = halves(conv_w)
    conv_b3 = halves(conv_b)[:, None, :]
    tabs = _rope_tabs(positions)
    sk = (sinks * LOG2E).reshape(depth, hkv_b, B_GROUP // 2, 2)
    rep = lambda a: jnp.repeat(a, TQ, axis=-1).reshape(depth, hkv_b, 1, (B_GROUP // 2) * TQ)
    sink_e, sink_o = rep(sk[..., 0]), rep(sk[..., 1])

    mod = _ada(c.reshape(d, 1), w_ada, b_ada)
    xs = x[0]
    for l in range(depth):
        sh_m, sc_m, gt_m, sh_f, sc_f, gt_f = [mod[l, :, n * d:(n + 1) * d] for n in range(N_MOD)]
        h = _norm_mod(xs, g_mix[l][None, :], sc_m, sh_m)
        pr = _proj_rope(h, w_r, l, tabs, lay)
        vat3, vbt3, wt = _proj_v(h, w_v, l, lay["n_va"], lay["n_vb"], lay["kc"])
        na = _dsa(pr, vat3, wt, g_out_a[l][None, :], lay, seq, n_sel)
        nb = _swa(pr, vbt3, sink_e[l], sink_o[l], g_out_b[l][None, :], lay, seq)
        xs = _oproj(na, nb, w_out_b, l, xs, gt_m)
        h = _norm_mod(xs, g_ffn[l][None, :], sc_f, sh_f)
        act = _up(h, w_up_b, l, conv_wp, conv_b3)
        xs = _down(act, w_down_b, l, xs, gt_f)
    return _final_norm(xs, g_final[None, :])[None]
```

```python
import functools
import math

import numpy as np
import jax
import jax.numpy as jnp
from jax import lax
from jax.experimental import pallas as pl
from jax.experimental.pallas import tpu as pltpu

A_HEAD_DIM = 128
A_GROUP = 4
IDX_HEADS = 16
IDX_DIM = 64
TOPK_MAX = 256
B_HEAD_DIM = 64
B_GROUP = 8
WINDOW = 128
CONV_WIDTH = 3
ROPE_THETA = 10000.0
EPS = 1e-6
NEG = -1e30
M_INIT = -1e29
N_MOD = 6
LOG2E = 1.4426950408889634

LANES = 128
VMEM_LIMIT = 56 * 1024 * 1024

TQ = 128
KC = 512
COUNT_UNROLL = 2

F32 = jnp.float32
BF16 = jnp.bfloat16
I32 = jnp.int32
I16 = jnp.int16

_NEG_BITS = int(np.float32(NEG).view(np.int32))
KEY_NEG = _NEG_BITS ^ ((_NEG_BITS >> 31) & 0x7FFFFFFF)
NEG_HI = KEY_NEG >> 16
NEG_LO = (KEY_NEG & 0xFFFF) - 32768
I16_MIN = -32768
I16_MAX = 32767


def _params(sem, vmem=VMEM_LIMIT):
    return pltpu.CompilerParams(dimension_semantics=sem, vmem_limit_bytes=vmem)


def _pick(n, prefs):
    for p in prefs:
        if n % p == 0:
            return p
    return n


def _dot_nt(a, b):
    return lax.dot_general(a, b, (((1,), (1,)), ((), ())), preferred_element_type=F32)


def _tree(op, parts):
    parts = list(parts)
    while len(parts) > 1:
        nxt = [op(parts[a], parts[a + 1]) for a in range(0, len(parts) - 1, 2)]
        if len(parts) % 2:
            nxt.append(parts[-1])
        parts = nxt
    return parts[0]


def _col_reduce(op, red, x):
    slabs = [x[r:r + 8] for r in range(0, x.shape[0], 8)]
    return red(_tree(op, slabs), axis=0, keepdims=True)


def _ada_kernel(c_ref, w_ref, b_ref, o_ref):
    k = pl.program_id(2)

    @pl.when(k == 0)
    def _():
        o_ref[...] = b_ref[...]

    c = c_ref[...]
    ca = c * (1.0 / (1.0 + jnp.exp(-c)))
    o_ref[...] += jnp.sum(w_ref[...] * ca, axis=0, keepdims=True)


def _ada(c_col, w_ada, b_ada):
    depth, d, n = w_ada.shape
    tk = _pick(d, (2048, 1024, 512, 256, 128))
    tn = _pick(n, (2048, 1024, 512, 256, 128))
    return pl.pallas_call(
        _ada_kernel,
        grid=(depth, n // tn, d // tk),
        in_specs=[
            pl.BlockSpec((tk, 1), lambda l, j, k: (k, 0)),
            pl.BlockSpec((None, tk, tn), lambda l, j, k: (l, k, j)),
            pl.BlockSpec((None, 1, tn), lambda l, j, k: (l, 0, j)),
        ],
        out_specs=pl.BlockSpec((None, 1, tn), lambda l, j, k: (l, 0, j)),
        out_shape=jax.ShapeDtypeStruct((depth, 1, n), F32),
        compiler_params=_params(("parallel", "parallel", "arbitrary")),
        name="ada",
    )(c_col, w_ada, b_ada.reshape(depth, 1, n))


def _norm_mod_kernel(x_ref, g_ref, sc_ref, sh_ref, o_ref):
    x = x_ref[...]
    ms = jnp.mean(x * x, axis=-1, keepdims=True)
    y = x * lax.rsqrt(ms + EPS) * g_ref[...]
    o_ref[...] = (y * (1.0 + sc_ref[...]) + sh_ref[...]).astype(o_ref.dtype)


def _norm_kernel(x_ref, g_ref, o_ref):
    x = x_ref[...]
    ms = jnp.mean(x * x, axis=-1, keepdims=True)
    o_ref[...] = (x * lax.rsqrt(ms + EPS) * g_ref[...]).astype(o_ref.dtype)


def _norm_mod(x, g, sc, sh):
    s, d = x.shape
    tm = _pick(s, (512, 256, 128))
    row = pl.BlockSpec((1, d), lambda i: (0, 0))
    return pl.pallas_call(
        _norm_mod_kernel,
        grid=(s // tm,),
        in_specs=[pl.BlockSpec((tm, d), lambda i: (i, 0)), row, row, row],
        out_specs=pl.BlockSpec((tm, d), lambda i: (i, 0)),
        out_shape=jax.ShapeDtypeStruct((s, d), BF16),
        compiler_params=_params(("parallel",)),
        name="norm_mod",
    )(x, g, sc, sh)


def _final_norm(x, g):
    s, d = x.shape
    tm = _pick(s, (512, 256, 128))
    return pl.pallas_call(
        _norm_kernel,
        grid=(s // tm,),
        in_specs=[pl.BlockSpec((tm, d), lambda i: (i, 0)), pl.BlockSpec((1, d), lambda i: (0, 0))],
        out_specs=pl.BlockSpec((tm, d), lambda i: (i, 0)),
        out_shape=jax.ShapeDtypeStruct((s, d), F32),
        compiler_params=_params(("parallel",)),
        name="final_norm",
    )(x, g)


def _proj_rope_kernel(h_ref, w_ref, tab_ref, o_ref, *, bounds, scales, tn, parts):
    j = pl.program_id(1)
    b_qa, b_qb, b_qi, b_ka = bounds
    s_qa, s_qb, s_qi = scales
    is_a = (j < b_qa) | ((j >= b_qi) & (j < b_ka))
    scale = jnp.where(j < b_qa, s_qa, jnp.where(j < b_qb, s_qb, jnp.where(j < b_qi, s_qi, 1.0))).astype(F32)
    n = h_ref.shape[0] // parts
    w = w_ref[...]

    def rope128(acc, r):
        c = tab_ref[r, 0:LANES] * scale
        s = tab_ref[r, LANES:2 * LANES] * scale
        for g in range(tn // LANES):
            xg = acc[:, g * LANES:(g + 1) * LANES]
            o_ref[r, g * LANES:(g + 1) * LANES] = (xg * c + pltpu.roll(xg, 64, 1) * s).astype(o_ref.dtype)

    def rope64(acc, r):
        c = tab_ref[r, 2 * LANES:3 * LANES] * scale
        shi = tab_ref[r, 3 * LANES:4 * LANES] * scale
        slo = tab_ref[r, 4 * LANES:5 * LANES] * scale
        for g in range(tn // LANES):
            xg = acc[:, g * LANES:(g + 1) * LANES]
            o_ref[r, g * LANES:(g + 1) * LANES] = (
                xg * c + pltpu.roll(xg, 32, 1) * shi + pltpu.roll(xg, 96, 1) * slo).astype(o_ref.dtype)

    def body(epilogue):
        for p in range(parts):
            r = slice(p * n, (p + 1) * n)
            epilogue(jnp.dot(h_ref[r, :], w, preferred_element_type=F32), r)

    pl.when(is_a)(lambda: body(rope128))
    pl.when(jnp.logical_not(is_a))(lambda: body(rope64))


def _proj_rope(h, w_r, layer, tabs, seg):
    s, d = h.shape
    n = w_r.shape[2]
    tn = seg["tn"]
    tm = _pick(s, (1024, 512, 256, 128))
    kern = functools.partial(_proj_rope_kernel, bounds=seg["bounds"], scales=seg["scales"], tn=tn,
                             parts=4 if tm % 512 == 0 else 1)
    return pl.pallas_call(
        kern,
        grid=(s // tm, n // tn),
        in_specs=[
            pl.BlockSpec((tm, d), lambda i, j: (i, 0)),
            pl.BlockSpec((None, d, tn), lambda i, j: (layer, 0, j)),
            pl.BlockSpec((tm, 5 * LANES), lambda i, j: (i, 0)),
        ],
        out_specs=pl.BlockSpec((tm, tn), lambda i, j: (i, j)),
        out_shape=jax.ShapeDtypeStruct((s, n), BF16),
        compiler_params=_params(("parallel", "arbitrary")),
        name="proj_rope",
    )(h, w_r, tabs)


VA_PAD = 16
VA_ROWS = A_HEAD_DIM + VA_PAD


def _proj_v_kernel(h_ref, w_ref, va_ref, vb_ref, wt_ref, *, n_va, n_vb, kc):
    acc = jnp.dot(h_ref[...], w_ref[...], preferred_element_type=F32)
    acct = acc.T
    tm = acc.shape[0]
    ones_rows = jnp.where(lax.broadcasted_iota(I32, (VA_PAD, kc), 0) == 0, 1.0, 0.0).astype(va_ref.dtype)
    for c in range(tm // kc):
        cols = slice(c * kc, (c + 1) * kc)
        for g in range(n_va // A_HEAD_DIM):
            va_ref[c, g * VA_ROWS:g * VA_ROWS + A_HEAD_DIM, :] = (
                acct[g * A_HEAD_DIM:(g + 1) * A_HEAD_DIM, cols].astype(va_ref.dtype))
            va_ref[c, g * VA_ROWS + A_HEAD_DIM:(g + 1) * VA_ROWS, :] = ones_rows
        vb_ref[c] = acct[n_va:n_va + n_vb, cols].astype(vb_ref.dtype)
    wt_ref[...] = acct[n_va + n_vb:n_va + n_vb + LANES] * (IDX_HEADS ** -0.5)


def _proj_v(h, w_v, layer, n_va, n_vb, kc):
    s, d = h.shape
    nv = w_v.shape[2]
    tm = _pick(s, (1024, 512))
    tm = max(tm, kc)
    n_va_aug = n_va // A_HEAD_DIM * VA_ROWS
    kern = functools.partial(_proj_v_kernel, n_va=n_va, n_vb=n_vb, kc=kc)
    return pl.pallas_call(
        kern,
        grid=(s // tm,),
        in_specs=[pl.BlockSpec((tm, d), lambda i: (i, 0)),
                  pl.BlockSpec((None, d, nv), lambda i: (layer, 0, 0))],
        out_specs=[
            pl.BlockSpec((tm // kc, n_va_aug, kc), lambda i: (i, 0, 0)),
            pl.BlockSpec((tm // kc, n_vb, kc), lambda i: (i, 0, 0)),
            pl.BlockSpec((LANES, tm), lambda i: (0, i)),
        ],
        out_shape=[
            jax.ShapeDtypeStruct((s // kc, n_va_aug, kc), BF16),
            jax.ShapeDtypeStruct((s // kc, n_vb, kc), BF16),
            jax.ShapeDtypeStruct((LANES, s), F32),
        ],
        compiler_params=_params(("parallel",)),
        name="proj_v",
    )(h, w_v)


def _dsa_kernel(qa_ref, qi_ref, wt_ref, ka_ref, ke_ref, vt_ref, g_ref, o_ref,
                key_ref, hi_ref, lo_ref, acc_ref, m_ref, st_ref, p_ref, al_ref, *, seq, n_sel, hkv, kc):
    i = pl.program_id(0)
    per = kc // TQ
    nchunk = (i + per) // per
    t_idx = i * TQ + lax.broadcasted_iota(I32, (1, TQ), 1)
    row_iota = lax.broadcasted_iota(I32, (kc, TQ), 0)
    n_pairs = IDX_HEADS // 2
    pairs_per_dot = 2
    n_dots = n_pairs // pairs_per_dot

    def rows(c):
        return pl.ds(c * kc if isinstance(c, int) else pl.multiple_of(c * kc, kc), kc)

    qp = [jnp.concatenate([qi_ref[:, (b * pairs_per_dot + p) * LANES:(b * pairs_per_dot + p + 1) * LANES]
                           for p in range(pairs_per_dot)], axis=0) for b in range(n_dots)]
    wrows = [wt_ref[h:h + 1, :] for h in range(IDX_HEADS)]

    def score_chunk(c, _):
        ke = ke_ref[rows(c), :]
        k_even, k_odd = ke[:, :LANES], ke[:, LANES:]
        logits = lambda b: (_dot_nt(k_even, qp[b]), _dot_nt(k_odd, qp[b]))
        cur = logits(0)
        sc = jnp.zeros((kc, TQ), F32)
        for b in range(n_dots):
            nxt = logits(b + 1) if b + 1 < n_dots else None
            for p in range(pairs_per_dot):
                h = 2 * (b * pairs_per_dot + p)
                sc = sc + wrows[h] * jnp.maximum(cur[0][:, p * TQ:(p + 1) * TQ], 0.0)
                sc = sc + wrows[h + 1] * jnp.maximum(cur[1][:, p * TQ:(p + 1) * TQ], 0.0)
            cur = nxt
        sc = jnp.where(c * kc + row_iota <= t_idx, sc, NEG)
        b32 = lax.bitcast_convert_type(sc, I32)
        key = b32 ^ (lax.shift_right_arithmetic(b32, 31) & 0x7FFFFFFF)
        key_ref[rows(c), :] = key
        hi_ref[rows(c), :] = lax.shift_right_arithmetic(key, 16).astype(I16)
        lo_ref[rows(c), :] = ((key & 0xFFFF) - 32768).astype(I16)
        return 0

    lax.fori_loop(0, nchunk, score_chunk, 0)

    unroll = math.gcd(COUNT_UNROLL, seq // kc)
    ntrip = (nchunk + unroll - 1) // unroll
    span = unroll * kc

    def fill_neg(c, _):
        hi_ref[rows(c), :] = jnp.full((kc, TQ), NEG_HI, I16)
        lo_ref[rows(c), :] = jnp.full((kc, TQ), NEG_LO, I16)
        return 0

    lax.fori_loop(nchunk, ntrip * unroll, fill_neg, 0)
    n_virtual = (seq - ntrip * span).astype(F32)
    cslab = 64

    def count16(ref, cand):
        def body(t, acc):
            r0 = pl.multiple_of(t * span, span)
            for u in range(unroll):
                v = ref[pl.ds(r0 + u * kc, kc), :]
                hit = jnp.where(v >= cand, jnp.int16(1), jnp.int16(0))
                acc = acc + _tree(jnp.add, [hit[r:r + cslab] for r in range(0, kc, cslab)])
            return acc
        part = lax.fori_loop(0, ntrip, body, jnp.zeros((cslab, TQ), I16))
        return _col_reduce(jnp.add, jnp.sum, part.astype(F32))

    def bisect16(count_fn, cnt_all):
        def step(it, carry):
            u, cnt_u = carry
            cand_u = u | lax.shift_left(jnp.int32(1), jnp.int32(15) - it)
            cnt = count_fn(cand_u - 32768)
            ok = cnt >= n_sel
            return jnp.where(ok, cand_u, u), jnp.where(ok, cnt, cnt_u)
        return lax.fori_loop(0, 16, step, (jnp.zeros((1, TQ), I32), cnt_all))

    def count_hi_ge(cand):
        return count16(hi_ref, cand.astype(I16)) + jnp.where(cand <= NEG_HI, n_virtual, 0.0)

    u_hi, cnt_hi_ge = bisect16(count_hi_ge, jnp.full((1, TQ), float(seq), F32))
    t_hi = u_hi - 32768
    t_hi16 = t_hi.astype(I16)
    def mask_lo(t, _):
        r0 = pl.multiple_of(t * span, span)
        for u in range(unroll):
            r = pl.ds(r0 + u * kc, kc)
            h = hi_ref[r, :]
            lo_ref[r, :] = jnp.where(h == t_hi16, lo_ref[r, :],
                                     jnp.where(h > t_hi16, jnp.int16(I16_MAX), jnp.int16(I16_MIN)))
        return 0

    lax.fori_loop(0, ntrip, mask_lo, 0)
    virt_gt = jnp.where(t_hi < NEG_HI, n_virtual, 0.0)
    virt_lo = jnp.where(t_hi == NEG_HI, n_virtual, 0.0)

    def count_lo_ge(cand):
        return count16(lo_ref, cand.astype(I16)) + virt_gt + jnp.where(cand <= NEG_LO, virt_lo, 0.0)

    u_lo, cnt_thr = bisect16(count_lo_ge, cnt_hi_ge)
    thr = lax.shift_left(t_hi, 16) | u_lo
    excess_ties = jnp.max(cnt_thr) > n_sel

    m_ref[...] = jnp.full(m_ref.shape, M_INIT, F32)
    acc_ref[...] = jnp.zeros(acc_ref.shape, F32)
    eye = jnp.where(lax.broadcasted_iota(I32, (TQ, LANES), 0) == lax.broadcasted_iota(I32, (TQ, LANES), 1),
                    1.0, 0.0).astype(BF16)
    rhs = [jnp.concatenate(
        [jnp.concatenate([qa_ref[:, (A_GROUP * g + h) * A_HEAD_DIM:(A_GROUP * g + h + 1) * A_HEAD_DIM], eye],
                         axis=1) for h in range(A_GROUP)], axis=0) for g in range(hkv)]

    def qk(c, bias_b, g):
        kg = ka_ref[rows(c), g * A_HEAD_DIM:(g + 1) * A_HEAD_DIM]
        return _dot_nt(jnp.concatenate([kg, bias_b], axis=1), rhs[g])

    def attend_chunk(c, bias_b, st, next_first):
        for g in range(hkv):
            if g + 1 < hkv:
                st_next = qk(c, bias_b, g + 1)
            else:
                st_next = next_first() if next_first is not None else None
            m_old = m_ref[g]
            m_new = jnp.maximum(m_old, _col_reduce(jnp.maximum, jnp.max, st))
            alpha = jnp.exp2(m_old - m_new)
            p = jnp.exp2(st - m_new).astype(BF16)
            m_ref[g] = m_new
            vt = vt_ref[c, g * VA_ROWS:(g + 1) * VA_ROWS, :]
            acc_ref[g] = acc_ref[g] * alpha + jnp.dot(vt, p, preferred_element_type=F32)
            st = st_next
        return st

    def plain_bias(c):
        return jnp.where(key_ref[rows(c), :] >= thr, 0.0, NEG).astype(BF16)

    def pv_update(c, g, p, alpha):
        vt = vt_ref[c, g * VA_ROWS:(g + 1) * VA_ROWS, :]
        acc_ref[g] = acc_ref[g] * alpha + jnp.dot(vt, p, preferred_element_type=F32)

    def attend_plain(c, _):
        st, p_prev, alpha_prev = st_ref[...], p_ref[...], al_ref[...]
        c_next = jnp.minimum(c + 1, nchunk - 1)
        c_prev = jnp.maximum(c - 1, 0)
        bias_b = plain_bias(c)
        for g in range(hkv):
            st_next = qk(c, bias_b, g + 1) if g + 1 < hkv else qk(c_next, plain_bias(c_next), 0)
            if g == 0:
                pv_update(c_prev, hkv - 1, p_prev, alpha_prev)
            else:
                pv_update(c, g - 1, p_prev, alpha_prev)
            m_old = m_ref[g]
            m_new = jnp.maximum(m_old, _col_reduce(jnp.maximum, jnp.max, st))
            alpha_prev = jnp.exp2(m_old - m_new)
            p_prev = jnp.exp2(st - m_new).astype(BF16)
            m_ref[g] = m_new
            st = st_next
        st_ref[...] = st
        p_ref[...] = p_prev
        al_ref[...] = alpha_prev
        return 0

    def run_ties():
        def count_gt(c, acc):
            return acc + _col_reduce(jnp.add, jnp.sum, jnp.where(key_ref[rows(c), :] > thr, 1.0, 0.0))
        n_neg_only = (seq - nchunk * kc).astype(F32)
        n_tie = n_sel - (lax.fori_loop(0, nchunk, count_gt, jnp.zeros((1, TQ), F32))
                         + jnp.where(thr < KEY_NEG, n_neg_only, 0.0))

        def attend_ties(c, tie_seen):
            k = key_ref[rows(c), :]
            ltri = jnp.where(lax.broadcasted_iota(I32, (kc, kc), 0) > lax.broadcasted_iota(I32, (kc, kc), 1),
                             1.0, 0.0).astype(BF16)
            eq = jnp.where(k == thr, 1.0, 0.0)
            rank = jnp.dot(ltri, eq.astype(BF16), preferred_element_type=F32) + tie_seen
            keep = jnp.where(k > thr, 1.0, jnp.where(rank < n_tie, eq, 0.0))
            keep = jnp.where(c * kc + row_iota <= t_idx, keep, 0.0)
            bias_b = jnp.where(keep > 0.5, 0.0, NEG).astype(BF16)
            attend_chunk(c, bias_b, qk(c, bias_b, 0), None)
            return tie_seen + _col_reduce(jnp.add, jnp.sum, eq)

        lax.fori_loop(0, nchunk, attend_ties, jnp.zeros((1, TQ), F32))

    def run_plain():
        st_ref[...] = qk(0, plain_bias(0), 0)
        p_ref[...] = jnp.zeros(p_ref.shape, BF16)
        al_ref[...] = jnp.ones(al_ref.shape, F32)
        lax.fori_loop(0, nchunk, attend_plain, 0)
        pv_update(nchunk - 1, hkv - 1, p_ref[...], al_ref[...])

    lax.cond(excess_ties, run_ties, run_plain)

    ssq = jnp.zeros((1, TQ), F32)
    for g in range(hkv):
        o = acc_ref[g, 0:A_HEAD_DIM, :] / acc_ref[g, A_HEAD_DIM:A_HEAD_DIM + 1, :]
        acc_ref[g, 0:A_HEAD_DIM, :] = o
        sq = jnp.sum(o * o, axis=0, keepdims=True)
        for h in range(A_GROUP):
            ssq = ssq + sq[:, h * TQ:(h + 1) * TQ]
    rn = lax.rsqrt(ssq / (hkv * A_GROUP * A_HEAD_DIM) + EPS)
    for g in range(hkv):
        for h in range(A_GROUP):
            col = (A_GROUP * g + h) * A_HEAD_DIM
            oh = acc_ref[g, 0:A_HEAD_DIM, h * TQ:(h + 1) * TQ] * rn
            o_ref[:, col:col + A_HEAD_DIM] = (oh.T * g_ref[:, col:col + A_HEAD_DIM]).astype(o_ref.dtype)


def _dsa(pr, vat3, wt, g_a, lay, seq, n_sel):
    d_a = lay["d_a"]
    hkv = lay["hkv_a"]
    nka = hkv * A_HEAD_DIM
    kc = lay["kc"]
    assert TQ == LANES
    one = pl.Buffered(1)
    kern = functools.partial(_dsa_kernel, seq=seq, n_sel=float(n_sel), hkv=hkv, kc=kc)
    return pl.pallas_call(
        kern,
        grid=(seq // TQ,),
        in_specs=[
            pl.BlockSpec((TQ, d_a), lambda i: (i, 0)),
            pl.BlockSpec((TQ, IDX_HEADS * IDX_DIM), lambda i: (i, lay["off_qi"] // (IDX_HEADS * IDX_DIM))),
            pl.BlockSpec((LANES, TQ), lambda i: (0, i)),
            pl.BlockSpec((seq, nka), lambda i: (0, lay["off_ka"] // nka), pipeline_mode=one),
            pl.BlockSpec((seq, 2 * LANES), lambda i: (0, lay["off_misc"] // (2 * LANES)), pipeline_mode=one),
            pl.BlockSpec((seq // kc, hkv * VA_ROWS, kc), lambda i: (0, 0, 0), pipeline_mode=one),
            pl.BlockSpec((1, d_a), lambda i: (0, 0)),
        ],
        out_specs=pl.BlockSpec((TQ, d_a), lambda i: (i, 0)),
        out_shape=jax.ShapeDtypeStruct((seq, d_a), BF16),
        scratch_shapes=[
            pltpu.VMEM((seq, TQ), I32),
            pltpu.VMEM((seq, TQ), I16),
            pltpu.VMEM((seq, TQ), I16),
            pltpu.VMEM((hkv, VA_ROWS, A_GROUP * TQ), F32),
            pltpu.VMEM((hkv, 1, A_GROUP * TQ), F32),
            pltpu.VMEM((kc, A_GROUP * TQ), F32),
            pltpu.VMEM((kc, A_GROUP * TQ), BF16),
            pltpu.VMEM((1, A_GROUP * TQ), F32),
        ],
        compiler_params=_params(("arbitrary",)),
        name="dsa",
    )(pr, pr, wt, pr, pr, vat3, g_a)


def _swa_kernel(qb_ref, kp_ref, kc_ref, vp_ref, vc_ref, se_ref, so_ref, g_ref, o_ref, ot_ref, *, hkv):
    i = pl.program_id(0)
    npair = B_GROUP // 2
    kwin = jnp.concatenate([kp_ref[...], kc_ref[...]], axis=0).astype(F32)
    r = lax.broadcasted_iota(I32, (2 * WINDOW, TQ), 0) - WINDOW
    q = lax.broadcasted_iota(I32, (2 * WINDOW, TQ), 1)
    ok = (r <= q) & (q - r < WINDOW) & (i * WINDOW + r >= 0)
    bias_b = jnp.where(ok, 0.0, NEG).astype(BF16)
    eye = jnp.where(lax.broadcasted_iota(I32, (TQ, LANES), 0) == lax.broadcasted_iota(I32, (TQ, LANES), 1),
                    1.0, 0.0).astype(BF16)
    ones_rows = jnp.where(lax.broadcasted_iota(I32, (VA_PAD, 2 * WINDOW), 0) == 0, 1.0, 0.0).astype(BF16)
    lane = lax.broadcasted_iota(I32, (2 * WINDOW, LANES), 1)
    nq = npair * TQ
    ssq = jnp.zeros((1, TQ), F32)
    for g in range(hkv):
        kcol = kwin[:, (g // 2) * LANES:(g // 2 + 1) * LANES]
        rolled = pltpu.roll(kcol, 64, 1)
        lo_src, hi_src = (kcol, rolled) if g % 2 == 0 else (rolled, kcol)
        k_even = jnp.where(lane < 64, lo_src, 0.0).astype(BF16)
        k_odd = jnp.where(lane >= 64, hi_src, 0.0).astype(BF16)
        lhs = jnp.concatenate([jnp.concatenate([k_even, bias_b], axis=1),
                               jnp.concatenate([k_odd, bias_b], axis=1)], axis=0)
        rhs = jnp.concatenate(
            [jnp.concatenate([qb_ref[:, (npair * g + p) * LANES:(npair * g + p + 1) * LANES], eye], axis=1)
             for p in range(npair)], axis=0)
        st = _dot_nt(lhs, rhs)
        ps, ms = [], []
        for half, s_ref in ((0, se_ref), (1, so_ref)):
            sth = st[half * 2 * WINDOW:(half + 1) * 2 * WINDOW]
            m = jnp.maximum(_col_reduce(jnp.maximum, jnp.max, sth), s_ref[g])
            ps.append(jnp.exp2(sth - m).astype(BF16))
            ms.append(m)
        v_aug = jnp.concatenate(
            [jnp.concatenate([vp_ref[g * B_HEAD_DIM:(g + 1) * B_HEAD_DIM, :],
                              vc_ref[g * B_HEAD_DIM:(g + 1) * B_HEAD_DIM, :]], axis=1), ones_rows], axis=0)
        ot = jnp.dot(v_aug, jnp.concatenate(ps, axis=1), preferred_element_type=F32)
        outs = []
        for half, s_ref in ((0, se_ref), (1, so_ref)):
            den = ot[B_HEAD_DIM:B_HEAD_DIM + 1, half * nq:(half + 1) * nq] + jnp.exp2(s_ref[g] - ms[half])
            outs.append(ot[0:B_HEAD_DIM, half * nq:(half + 1) * nq] / den)
        for p in range(npair):
            blk = jnp.concatenate([outs[0][:, p * TQ:(p + 1) * TQ], outs[1][:, p * TQ:(p + 1) * TQ]],
                                  axis=0)
            ot_ref[npair * g + p] = blk
            ssq = ssq + jnp.sum(blk * blk, axis=0, keepdims=True)
    rn = lax.rsqrt(ssq / (hkv * B_GROUP * B_HEAD_DIM) + EPS)
    for c in range(hkv * npair):
        o_ref[:, c * LANES:(c + 1) * LANES] = (
            (ot_ref[c] * rn).T * g_ref[:, c * LANES:(c + 1) * LANES]).astype(o_ref.dtype)


def _swa(pr, vbt3, sink_e, sink_o, g_b, lay, seq):
    d_b = lay["d_b"]
    hkv = lay["hkv_b"]
    kc = lay["kc"]
    per = kc // TQ
    nvb = hkv * B_HEAD_DIM
    off_kb = lay["off_misc"] + 2 * LANES
    assert nvb % LANES == 0 and off_kb % nvb == 0
    col_kb = off_kb // nvb
    prev = lambda i: jnp.maximum(i - 1, 0)
    kern = functools.partial(_swa_kernel, hkv=hkv)
    return pl.pallas_call(
        kern,
        grid=(seq // TQ,),
        in_specs=[
            pl.BlockSpec((TQ, d_b), lambda i: (i, lay["off_qb"] // d_b)),
            pl.BlockSpec((TQ, nvb), lambda i: (prev(i), col_kb)),
            pl.BlockSpec((TQ, nvb), lambda i: (i, col_kb)),
            pl.BlockSpec((None, nvb, TQ), lambda i: (prev(i) // per, 0, prev(i) % per)),
            pl.BlockSpec((None, nvb, TQ), lambda i: (i // per, 0, i % per)),
            pl.BlockSpec((hkv, 1, (B_GROUP // 2) * TQ), lambda i: (0, 0, 0)),
            pl.BlockSpec((hkv, 1, (B_GROUP // 2) * TQ), lambda i: (0, 0, 0)),
            pl.BlockSpec((1, d_b), lambda i: (0, 0)),
        ],
        out_specs=pl.BlockSpec((TQ, d_b), lambda i: (i, 0)),
        out_shape=jax.ShapeDtypeStruct((seq, d_b), BF16),
        scratch_shapes=[pltpu.VMEM((hkv * B_GROUP // 2, LANES, TQ), F32)],
        compiler_params=_params(("parallel",)),
        name="swa",
    )(pr, pr, pr, vbt3, vbt3, sink_e, sink_o, g_b)


def _oproj_kernel(na_ref, nb_ref, wa_ref, wb_ref, x_ref, gt_ref, o_ref):
    acc = jnp.dot(na_ref[...], wa_ref[...], preferred_element_type=F32)
    acc = acc + jnp.dot(nb_ref[...], wb_ref[...], preferred_element_type=F32)
    o_ref[...] = x_ref[...] + gt_ref[...] * acc


def _oproj(na, nb, w_out, layer, x, gate):
    s, d_a = na.shape
    d_b = nb.shape[1]
    d = x.shape[1]
    assert d_a == d_b
    tm = _pick(s, (1024, 512, 256, 128))
    tn = _pick(d, (512, 256, 128))
    return pl.pallas_call(
        _oproj_kernel,
        grid=(s // tm, d // tn),
        in_specs=[
            pl.BlockSpec((tm, d_a), lambda i, j: (i, 0)),
            pl.BlockSpec((tm, d_b), lambda i, j: (i, 0)),
            pl.BlockSpec((None, d_a, tn), lambda i, j: (layer, 0, j)),
            pl.BlockSpec((None, d_b, tn), lambda i, j: (layer, 1, j)),
            pl.BlockSpec((tm, tn), lambda i, j: (i, j)),
            pl.BlockSpec((1, tn), lambda i, j: (0, j)),
        ],
        out_specs=pl.BlockSpec((tm, tn), lambda i, j: (i, j)),
        out_shape=jax.ShapeDtypeStruct((s, d), F32),
        compiler_params=_params(("parallel", "arbitrary")),
        name="oproj",
    )(na, nb, w_out, w_out, x, gate)


HALO = 8
UP_PARTS = 1
FF_ALIGN = 512


def _up_kernel(h_ref, wg_ref, wv_ref, cwg_ref, cwv_ref, cbg_ref, cbv_ref, o_ref, eg_ref, ev_ref, *, tm, parts):
    i = pl.program_id(1)

    @pl.when(i == 0)
    def _():
        eg_ref[0:HALO, :] = jnp.zeros((HALO, eg_ref.shape[1]), F32)
        ev_ref[0:HALO, :] = jnp.zeros((HALO, ev_ref.shape[1]), F32)

    @pl.when(i > 0)
    def _():
        eg_ref[0:HALO, :] = eg_ref[tm:tm + HALO, :]
        ev_ref[0:HALO, :] = ev_ref[tm:tm + HALO, :]

    def conv(e_ref, cw_ref, cb_ref, r0, n):
        y = cb_ref[...] + cw_ref[CONV_WIDTH - 1:CONV_WIDTH, :] * e_ref[HALO + r0:HALO + r0 + n, :]
        for k in range(1, CONV_WIDTH):
            y = y + cw_ref[CONV_WIDTH - 1 - k:CONV_WIDTH - k, :] * e_ref[HALO + r0 - k:HALO + r0 - k + n, :]
        return y

    n = tm // parts
    for p in range(parts):
        h = h_ref[p * n:(p + 1) * n, :]
        eg_ref[HALO + p * n:HALO + (p + 1) * n, :] = jnp.dot(h, wg_ref[...], preferred_element_type=F32)
        ev_ref[HALO + p * n:HALO + (p + 1) * n, :] = jnp.dot(h, wv_ref[...], preferred_element_type=F32)
    for p in range(parts):
        gte = conv(eg_ref, cwg_ref, cbg_ref, p * n, n)
        val = conv(ev_ref, cwv_ref, cbv_ref, p * n, n)
        o_ref[p * n:(p + 1) * n, :] = (gte * (1.0 / (1.0 + jnp.exp(-gte))) * val).astype(o_ref.dtype)


def _up(h, w_up, layer, conv_w, conv_b):
    s, d = h.shape
    f = w_up.shape[2] // 2
    tn = _pick(f, (512, 256, 128))
    tm = _pick(s, (1024, 512, 256, 128))
    nf = f // tn
    kern = functools.partial(_up_kernel, tm=tm, parts=UP_PARTS if tm % (UP_PARTS * 128) == 0 else 1)
    gate = lambda j, i: (layer, 0, j)
    val = lambda j, i: (layer, 0, nf + j)
    return pl.pallas_call(
        kern,
        grid=(nf, s // tm),
        in_specs=[
            pl.BlockSpec((tm, d), lambda j, i: (i, 0)),
            pl.BlockSpec((None, d, tn), gate), pl.BlockSpec((None, d, tn), val),
            pl.BlockSpec((None, CONV_WIDTH, tn), gate), pl.BlockSpec((None, CONV_WIDTH, tn), val),
            pl.BlockSpec((None, 1, tn), gate), pl.BlockSpec((None, 1, tn), val),
        ],
        out_specs=pl.BlockSpec((tm, tn), lambda j, i: (i, j)),
        out_shape=jax.ShapeDtypeStruct((s, f), BF16),
        scratch_shapes=[pltpu.VMEM((tm + HALO, tn), F32), pltpu.VMEM((tm + HALO, tn), F32)],
        compiler_params=_params(("arbitrary", "arbitrary")),
        name="up_conv_gate",
    )(h, w_up, w_up, conv_w, conv_w, conv_b, conv_b)


def _down_kernel(a_ref, w_ref, x_ref, gt_ref, o_ref):
    o_ref[...] = x_ref[...] + gt_ref[...] * jnp.dot(a_ref[...], w_ref[...], preferred_element_type=F32)


def _down(act, wd, layer, x, gate):
    s, f = act.shape
    d = x.shape[1]
    tm = _pick(s, (512, 256, 128))
    tn = _pick(d, (512, 256, 128))
    return pl.pallas_call(
        _down_kernel,
        grid=(s // tm, d // tn),
        in_specs=[
            pl.BlockSpec((tm, f), lambda i, j: (i, 0)),
            pl.BlockSpec((None, f, tn), lambda i, j: (layer, 0, j)),
            pl.BlockSpec((tm, tn), lambda i, j: (i, j)),
            pl.BlockSpec((1, tn), lambda i, j: (0, j)),
        ],
        out_specs=pl.BlockSpec((tm, tn), lambda i, j: (i, j)),
        out_shape=jax.ShapeDtypeStruct((s, d), F32),
        compiler_params=_params(("parallel", "arbitrary")),
        name="down",
    )(act, wd, x, gate)


def _cast_pad_kernel(x_ref, o_ref, *, axis, n_real, n_out):
    r = pl.program_id(axis) % n_out

    @pl.when(r < n_real)
    def _():
        o_ref[...] = x_ref[...].astype(o_ref.dtype)

    @pl.when(r >= n_real)
    def _():
        o_ref[...] = jnp.zeros(o_ref.shape, o_ref.dtype)


def _pad_unit(f, padf):
    g = np.gcd(f, padf) if padf else f
    return _pick(int(g), (512, 256, 128))


def _cast_up(w_up, f, padf):
    depth, d, _ = w_up.shape
    cw = _pad_unit(f, padf)
    n_real, n_out = f // cw, (f + padf) // cw
    src = lambda l, t: (l, 0, (t // n_out) * n_real + jnp.minimum(t % n_out, n_real - 1))
    kern = functools.partial(_cast_pad_kernel, axis=1, n_real=n_real, n_out=n_out)
    return pl.pallas_call(
        kern,
        grid=(depth, 2 * n_out),
        in_specs=[pl.BlockSpec((None, d, cw), src)],
        out_specs=pl.BlockSpec((None, d, cw), lambda l, t: (l, 0, t)),
        out_shape=jax.ShapeDtypeStruct((depth, d, 2 * (f + padf)), BF16),
        compiler_params=_params(("parallel", "arbitrary")),
        name="cast_up",
    )(w_up)


def _cast_down(w_down, padf):
    depth, f, d = w_down.shape
    rt = _pad_unit(f, padf)
    n_real, n_out = f // rt, (f + padf) // rt
    kern = functools.partial(_cast_pad_kernel, axis=1, n_real=n_real, n_out=n_out)
    return pl.pallas_call(
        kern,
        grid=(depth, n_out),
        in_specs=[pl.BlockSpec((None, rt, d), lambda l, t: (l, jnp.minimum(t, n_real - 1), 0))],
        out_specs=pl.BlockSpec((None, rt, d), lambda l, t: (l, t, 0)),
        out_shape=jax.ShapeDtypeStruct((depth, f + padf, d), BF16),
        compiler_params=_params(("parallel", "arbitrary")),
        name="cast_down",
    )(w_down)


def _layout(d_model, seq):
    d_a = d_model // 2
    d_b = d_model - d_a
    ha = d_a // A_HEAD_DIM
    hkv_a = ha // A_GROUP
    hb = d_b // B_HEAD_DIM
    hkv_b = hb // B_GROUP
    n_qi = IDX_HEADS * IDX_DIM
    n_ka = hkv_a * A_HEAD_DIM
    n_kb = hkv_b * B_HEAD_DIM
    misc_raw = 2 * LANES + n_kb
    seg_w = [d_a, d_b, n_qi, n_ka]
    tn = 512
    while any(w % tn for w in seg_w) or tn > misc_raw + LANES:
        tn //= 2
    misc_w = -(-misc_raw // tn) * tn
    off_qb = d_a
    off_qi = off_qb + d_b
    off_ka = off_qi + n_qi
    off_misc = off_ka + n_ka
    n_r = off_misc + misc_w
    bounds = (off_qb // tn, off_qi // tn, off_ka // tn, off_misc // tn)
    scales = (A_HEAD_DIM ** -0.5 * LOG2E, B_HEAD_DIM ** -0.5 * LOG2E, IDX_DIM ** -0.5)
    kc = min(KC, seq)
    assert d_a == d_b and seq % kc == 0 and kc % TQ == 0 and off_qi % n_qi == 0 and off_ka % n_ka == 0
    assert off_misc % (2 * LANES) == 0
    return dict(d_a=d_a, d_b=d_b, ha=ha, hkv_a=hkv_a, hb=hb, hkv_b=hkv_b, n_qi=n_qi, n_ka=n_ka, n_kb=n_kb,
                tn=tn, misc_w=misc_w, off_qb=off_qb, off_qi=off_qi, off_ka=off_ka, off_misc=off_misc,
                n_r=n_r, bounds=bounds, scales=scales, kc=kc, n_va=n_ka, n_vb=n_kb)


M_COPY, M_KI_EVEN, M_KI_ODD, M_ZERO, M_WI = range(5)
ROW_ALIGN = 16


def _relayout_kernel(off, mode, a_ref, o_ref):
    m = mode[pl.program_id(1)]
    lane = lax.broadcasted_iota(I32, o_ref.shape, 1)

    def put(fn):
        def _():
            o_ref[...] = fn(a_ref[0].T).astype(o_ref.dtype)
        return _

    pl.when(m == M_COPY)(put(lambda x: x))
    pl.when(m == M_KI_EVEN)(put(lambda x: jnp.where(lane < IDX_DIM, x, 0.0)))
    pl.when(m == M_KI_ODD)(put(lambda x: jnp.where(lane >= IDX_DIM, pltpu.roll(x, IDX_DIM, 1), 0.0)))
    pl.when(m == M_ZERO)(put(lambda x: jnp.zeros_like(x)))
    pl.when(m == M_WI)(put(lambda x: jnp.where(lane < IDX_HEADS, pltpu.roll(x, LANES - IDX_DIM, 1), 0.0)))


def _relayout(w_in_t, tiles):
    depth, n_src, d = w_in_t.shape
    offs = [col for _, col in tiles]
    assert all(o % ROW_ALIGN == 0 and o + LANES <= n_src for o in offs)
    as_i32 = lambda v: jnp.asarray(np.asarray(v, np.int32))
    n_tiles = len(tiles)
    window = pl.BlockSpec((pl.Element(1), pl.Element(LANES), pl.Element(d)),
                          lambda l, t, off, m: (l, pl.multiple_of(off[t], ROW_ALIGN), 0))
    return pl.pallas_call(
        _relayout_kernel,
        grid_spec=pltpu.PrefetchScalarGridSpec(
            num_scalar_prefetch=2,
            grid=(depth, n_tiles),
            in_specs=[window],
            out_specs=pl.BlockSpec((None, d, LANES), lambda l, t, off, m: (l, 0, t)),
        ),
        out_shape=jax.ShapeDtypeStruct((depth, d, n_tiles * LANES), BF16),
        compiler_params=_params(("parallel", "arbitrary")),
        name="relayout_w_in",
    )(as_i32(offs), as_i32([m for m, _ in tiles]), w_in_t)


def _prep_w_in(w_in, lay):
    d_a, d_b = lay["d_a"], lay["d_b"]
    widths = (d_a, lay["n_ka"], lay["n_ka"], lay["n_qi"], IDX_DIM, IDX_HEADS, d_b, lay["n_kb"], lay["n_kb"])
    o_qa, o_ka, o_va, o_qi, o_ki, o_wi, o_qb, o_kb, o_vb = [int(v) for v in np.cumsum((0,) + widths[:-1])]
    assert o_wi == o_ki + IDX_DIM and 2 * IDX_DIM == LANES
    seg = lambda off, width: [(M_COPY, off + k * LANES) for k in range(width // LANES)]
    pad_m = lay["misc_w"] - (2 * LANES + lay["n_kb"])
    w_in_t = jnp.swapaxes(w_in, 1, 2)
    w_r = _relayout(w_in_t, seg(o_qa, d_a) + seg(o_qb, d_b) + seg(o_qi, lay["n_qi"]) + seg(o_ka, lay["n_ka"])
                    + [(M_KI_EVEN, o_ki), (M_KI_ODD, o_ki)] + seg(o_kb, lay["n_kb"])
                    + [(M_ZERO, 0)] * (pad_m // LANES))
    w_v = _relayout(w_in_t, seg(o_va, lay["n_va"]) + seg(o_vb, lay["n_vb"]) + [(M_WI, o_ki)])
    return w_r, w_v


def _rope_tabs(positions):
    pos = positions[0].astype(F32)
    inv_a = ROPE_THETA ** (-jnp.arange(0, A_HEAD_DIM, 2, dtype=F32) / A_HEAD_DIM)
    ang_a = pos[:, None] * inv_a
    ca, sa = jnp.cos(ang_a), jnp.sin(ang_a)
    inv_6 = ROPE_THETA ** (-jnp.arange(0, B_HEAD_DIM, 2, dtype=F32) / B_HEAD_DIM)
    ang_6 = pos[:, None] * inv_6
    c6, s6 = jnp.cos(ang_6), jnp.sin(ang_6)
    z6 = jnp.zeros_like(s6)
    return jnp.concatenate([
        ca, ca, -sa, sa,
        c6, c6, c6, c6,
        z6, s6, z6, s6,
        -s6, z6, -s6, z6], axis=-1)


def kernel(x, c, positions, w_ada, b_ada, g_mix, w_in, g_out_a, g_out_b, sinks, w_out, g_ffn, w_up,
           conv_w, conv_b, w_down, g_final):
    b, seq, d = x.shape
    assert b == 1 and IDX_DIM == B_HEAD_DIM
    depth = w_ada.shape[0]
    lay = _layout(d, seq)
    n_sel = min(TOPK_MAX, seq // 4)
    hkv_b = lay["hkv_b"]

    w_r, w_v = _prep_w_in(w_in, lay)
    w_out_b = w_out.astype(BF16)
    f = w_down.shape[1]
    padf = -f % FF_ALIGN
    halves = lambda a: jnp.concatenate(
        [a[..., :f], jnp.zeros(a.shape[:-1] + (padf,), a.dtype), a[..., f:],
         jnp.zeros(a.shape[:-1] + (padf,), a.dtype)], axis=-1)
    w_up_b = _cast_up(w_up, f, padf)
    w_down_b = _cast_down(w_down, padf)
    conv_wp = halves(conv_w)
    conv_b3 = halves(conv_b)[:, None, :]
    tabs = _rope_tabs(positions)
    sk = (sinks * LOG2E).reshape(depth, hkv_b, B_GROUP // 2, 2)
    rep = lambda a: jnp.repeat(a, TQ, axis=-1).reshape(depth, hkv_b, 1, (B_GROUP // 2) * TQ)
    sink_e, sink_o = rep(sk[..., 0]), rep(sk[..., 1])

    mod = _ada(c.reshape(d, 1), w_ada, b_ada)
    xs = x[0]
    for l in range(depth):
        sh_m, sc_m, gt_m, sh_f, sc_f, gt_f = [mod[l, :, n * d:(n + 1) * d] for n in range(N_MOD)]
        h = _norm_mod(xs, g_mix[l][None, :], sc_m, sh_m)
        pr = _proj_rope(h, w_r, l, tabs, lay)
        vat3, vbt3, wt = _proj_v(h, w_v, l, lay["n_va"], lay["n_vb"], lay["kc"])
        na = _dsa(pr, vat3, wt, g_out_a[l][None, :], lay, seq, n_sel)
        nb = _swa(pr, vbt3, sink_e[l], sink_o[l], g_out_b[l][None, :], lay, seq)
        xs = _oproj(na, nb, w_out_b, l, xs, gt_m)
        h = _norm_mod(xs, g_ffn[l][None, :], sc_f, sh_f)
        act = _up(h, w_up_b, l, conv_wp, conv_b3)
        xs = _down(act, w_down_b, l, xs, gt_f)
    return _final_norm(xs, g_final[None, :])[None]
```

```python
import functools
import math

import numpy as np
import jax
import jax.numpy as jnp
from jax import lax
from jax.experimental import pallas as pl
from jax.experimental.pallas import tpu as pltpu

A_HEAD_DIM = 128
A_GROUP = 4
IDX_HEADS = 16
IDX_DIM = 64
TOPK_MAX = 256
B_HEAD_DIM = 64
B_GROUP = 8
WINDOW = 128
CONV_WIDTH = 3
ROPE_THETA = 10000.0
EPS = 1e-6
NEG = -1e30
M_INIT = -1e29
N_MOD = 6
LOG2E = 1.4426950408889634

LANES = 128
VMEM_LIMIT = 56 * 1024 * 1024

TQ = 128
KC = 512
COUNT_UNROLL = 2

F32 = jnp.float32
BF16 = jnp.bfloat16
I32 = jnp.int32
I16 = jnp.int16

_NEG_BITS = int(np.float32(NEG).view(np.int32))
KEY_NEG = _NEG_BITS ^ ((_NEG_BITS >> 31) & 0x7FFFFFFF)
NEG_HI = KEY_NEG >> 16
NEG_LO = (KEY_NEG & 0xFFFF) - 32768
I16_MIN = -32768
I16_MAX = 32767


def _params(sem, vmem=VMEM_LIMIT):
    return pltpu.CompilerParams(dimension_semantics=sem, vmem_limit_bytes=vmem)


def _pick(n, prefs):
    for p in prefs:
        if n % p == 0:
            return p
    return n


def _dot_nt(a, b):
    return lax.dot_general(a, b, (((1,), (1,)), ((), ())), preferred_element_type=F32)


def _tree(op, parts):
    parts = list(parts)
    while len(parts) > 1:
        nxt = [op(parts[a], parts[a + 1]) for a in range(0, len(parts) - 1, 2)]
        if len(parts) % 2:
            nxt.append(parts[-1])
        parts = nxt
    return parts[0]


def _col_reduce(op, red, x):
    slabs = [x[r:r + 8] for r in range(0, x.shape[0], 8)]
    return red(_tree(op, slabs), axis=0, keepdims=True)


def _ada_kernel(c_ref, w_ref, b_ref, o_ref):
    k = pl.program_id(2)

    @pl.when(k == 0)
    def _():
        o_ref[...] = b_ref[...]

    c = c_ref[...]
    ca = c * (1.0 / (1.0 + jnp.exp(-c)))
    o_ref[...] += jnp.sum(w_ref[...] * ca, axis=0, keepdims=True)


def _ada(c_col, w_ada, b_ada):
    depth, d, n = w_ada.shape
    tk = _pick(d, (2048, 1024, 512, 256, 128))
    tn = _pick(n, (2048, 1024, 512, 256, 128))
    return pl.pallas_call(
        _ada_kernel,
        grid=(depth, n // tn, d // tk),
        in_specs=[
            pl.BlockSpec((tk, 1), lambda l, j, k: (k, 0)),
            pl.BlockSpec((None, tk, tn), lambda l, j, k: (l, k, j)),
            pl.BlockSpec((None, 1, tn), lambda l, j, k: (l, 0, j)),
        ],
        out_specs=pl.BlockSpec((None, 1, tn), lambda l, j, k: (l, 0, j)),
        out_shape=jax.ShapeDtypeStruct((depth, 1, n), F32),
        compiler_params=_params(("parallel", "parallel", "arbitrary")),
        name="ada",
    )(c_col, w_ada, b_ada.reshape(depth, 1, n))


def _norm_mod_kernel(x_ref, g_ref, sc_ref, sh_ref, o_ref):
    x = x_ref[...]
    ms = jnp.mean(x * x, axis=-1, keepdims=True)
    y = x * lax.rsqrt(ms + EPS) * g_ref[...]
    o_ref[...] = (y * (1.0 + sc_ref[...]) + sh_ref[...]).astype(o_ref.dtype)


def _norm_kernel(x_ref, g_ref, o_ref):
    x = x_ref[...]
    ms = jnp.mean(x * x, axis=-1, keepdims=True)
    o_ref[...] = (x * lax.rsqrt(ms + EPS) * g_ref[...]).astype(o_ref.dtype)


def _norm_mod(x, g, sc, sh):
    s, d = x.shape
    tm = _pick(s, (512, 256, 128))
    row = pl.BlockSpec((1, d), lambda i: (0, 0))
    return pl.pallas_call(
        _norm_mod_kernel,
        grid=(s // tm,),
        in_specs=[pl.BlockSpec((tm, d), lambda i: (i, 0)), row, row, row],
        out_specs=pl.BlockSpec((tm, d), lambda i: (i, 0)),
        out_shape=jax.ShapeDtypeStruct((s, d), BF16),
        compiler_params=_params(("parallel",)),
        name="norm_mod",
    )(x, g, sc, sh)


def _final_norm(x, g):
    s, d = x.shape
    tm = _pick(s, (512, 256, 128))
    return pl.pallas_call(
        _norm_kernel,
        grid=(s // tm,),
        in_specs=[pl.BlockSpec((tm, d), lambda i: (i, 0)), pl.BlockSpec((1, d), lambda i: (0, 0))],
        out_specs=pl.BlockSpec((tm, d), lambda i: (i, 0)),
        out_shape=jax.ShapeDtypeStruct((s, d), F32),
        compiler_params=_params(("parallel",)),
        name="final_norm",
    )(x, g)


def _proj_rope_kernel(h_ref, w_ref, tab_ref, o_ref, *, bounds, scales, tn, parts):
    j = pl.program_id(1)
    b_qa, b_qb, b_qi, b_ka = bounds
    s_qa, s_qb, s_qi = scales
    is_a = (j < b_qa) | ((j >= b_qi) & (j < b_ka))
    scale = jnp.where(j < b_qa, s_qa, jnp.where(j < b_qb, s_qb, jnp.where(j < b_qi, s_qi, 1.0))).astype(F32)
    n = h_ref.shape[0] // parts
    w = w_ref[...]

    def rope128(acc, r):
        c = tab_ref[r, 0:LANES] * scale
        s = tab_ref[r, LANES:2 * LANES] * scale
        for g in range(tn // LANES):
            xg = acc[:, g * LANES:(g + 1) * LANES]
            o_ref[r, g * LANES:(g + 1) * LANES] = (xg * c + pltpu.roll(xg, 64, 1) * s).astype(o_ref.dtype)

    def rope64(acc, r):
        c = tab_ref[r, 2 * LANES:3 * LANES] * scale
        shi = tab_ref[r, 3 * LANES:4 * LANES] * scale
        slo = tab_ref[r, 4 * LANES:5 * LANES] * scale
        for g in range(tn // LANES):
            xg = acc[:, g * LANES:(g + 1) * LANES]
            o_ref[r, g * LANES:(g + 1) * LANES] = (
                xg * c + pltpu.roll(xg, 32, 1) * shi + pltpu.roll(xg, 96, 1) * slo).astype(o_ref.dtype)

    def body(epilogue):
        for p in range(parts):
            r = slice(p * n, (p + 1) * n)
            epilogue(jnp.dot(h_ref[r, :], w, preferred_element_type=F32), r)

    pl.when(is_a)(lambda: body(rope128))
    pl.when(jnp.logical_not(is_a))(lambda: body(rope64))


def _proj_rope(h, w_r, layer, tabs, seg):
    s, d = h.shape
    n = w_r.shape[2]
    tn = seg["tn"]
    tm = _pick(s, (1024, 512, 256, 128))
    kern = functools.partial(_proj_rope_kernel, bounds=seg["bounds"], scales=seg["scales"], tn=tn,
                             parts=4 if tm % 512 == 0 else 1)
    return pl.pallas_call(
        kern,
        grid=(s // tm, n // tn),
        in_specs=[
            pl.BlockSpec((tm, d), lambda i, j: (i, 0)),
            pl.BlockSpec((None, d, tn), lambda i, j: (layer, 0, j)),
            pl.BlockSpec((tm, 5 * LANES), lambda i, j: (i, 0)),
        ],
        out_specs=pl.BlockSpec((tm, tn), lambda i, j: (i, j)),
        out_shape=jax.ShapeDtypeStruct((s, n), BF16),
        compiler_params=_params(("parallel", "arbitrary")),
        name="proj_rope",
    )(h, w_r, tabs)


VA_PAD = 16
VA_ROWS = A_HEAD_DIM + VA_PAD


def _proj_v_kernel(h_ref, w_ref, va_ref, vb_ref, wt_ref, *, n_va, n_vb, kc):
    acc = jnp.dot(h_ref[...], w_ref[...], preferred_element_type=F32)
    acct = acc.T
    tm = acc.shape[0]
    ones_rows = jnp.where(lax.broadcasted_iota(I32, (VA_PAD, kc), 0) == 0, 1.0, 0.0).astype(va_ref.dtype)
    for c in range(tm // kc):
        cols = slice(c * kc, (c + 1) * kc)
        for g in range(n_va // A_HEAD_DIM):
            va_ref[c, g * VA_ROWS:g * VA_ROWS + A_HEAD_DIM, :] = (
                acct[g * A_HEAD_DIM:(g + 1) * A_HEAD_DIM, cols].astype(va_ref.dtype))
            va_ref[c, g * VA_ROWS + A_HEAD_DIM:(g + 1) * VA_ROWS, :] = ones_rows
        vb_ref[c] = acct[n_va:n_va + n_vb, cols].astype(vb_ref.dtype)
    wt_ref[...] = acct[n_va + n_vb:n_va + n_vb + LANES] * (IDX_HEADS ** -0.5)


def _proj_v(h, w_v, layer, n_va, n_vb, kc):
    s, d = h.shape
    nv = w_v.shape[2]
    tm = _pick(s, (1024, 512))
    tm = max(tm, kc)
    n_va_aug = n_va // A_HEAD_DIM * VA_ROWS
    kern = functools.partial(_proj_v_kernel, n_va=n_va, n_vb=n_vb, kc=kc)
    return pl.pallas_call(
        kern,
        grid=(s // tm,),
        in_specs=[pl.BlockSpec((tm, d), lambda i: (i, 0)),
                  pl.BlockSpec((None, d, nv), lambda i: (layer, 0, 0))],
        out_specs=[
            pl.BlockSpec((tm // kc, n_va_aug, kc), lambda i: (i, 0, 0)),
            pl.BlockSpec((tm // kc, n_vb, kc), lambda i: (i, 0, 0)),
            pl.BlockSpec((LANES, tm), lambda i: (0, i)),
        ],
        out_shape=[
            jax.ShapeDtypeStruct((s // kc, n_va_aug, kc), BF16),
            jax.ShapeDtypeStruct((s // kc, n_vb, kc), BF16),
            jax.ShapeDtypeStruct((LANES, s), F32),
        ],
        compiler_params=_params(("parallel",)),
        name="proj_v",
    )(h, w_v)


def _dsa_kernel(qa_ref, qi_ref, wt_ref, ka_ref, ke_ref, vt_ref, g_ref, o_ref,
                key_ref, hi_ref, lo_ref, acc_ref, m_ref, st_ref, p_ref, al_ref, *, seq, n_sel, hkv, kc):
    i = pl.program_id(0)
    per = kc // TQ
    nchunk = (i + per) // per
    t_idx = i * TQ + lax.broadcasted_iota(I32, (1, TQ), 1)
    row_iota = lax.broadcasted_iota(I32, (kc, TQ), 0)
    n_pairs = IDX_HEADS // 2
    pairs_per_dot = 2
    n_dots = n_pairs // pairs_per_dot

    def rows(c):
        return pl.ds(c * kc if isinstance(c, int) else pl.multiple_of(c * kc, kc), kc)

    qp = [jnp.concatenate([qi_ref[:, (b * pairs_per_dot + p) * LANES:(b * pairs_per_dot + p + 1) * LANES]
                           for p in range(pairs_per_dot)], axis=0) for b in range(n_dots)]
    wrows = [wt_ref[h:h + 1, :] for h in range(IDX_HEADS)]

    def score_chunk(c, _):
        ke = ke_ref[rows(c), :]
        k_even, k_odd = ke[:, :LANES], ke[:, LANES:]
        logits = lambda b: (_dot_nt(k_even, qp[b]), _dot_nt(k_odd, qp[b]))
        cur = logits(0)
        sc = jnp.zeros((kc, TQ), F32)
        for b in range(n_dots):
            nxt = logits(b + 1) if b + 1 < n_dots else None
            for p in range(pairs_per_dot):
                h = 2 * (b * pairs_per_dot + p)
                sc = sc + wrows[h] * jnp.maximum(cur[0][:, p * TQ:(p + 1) * TQ], 0.0)
                sc = sc + wrows[h + 1] * jnp.maximum(cur[1][:, p * TQ:(p + 1) * TQ], 0.0)
            cur = nxt
        sc = jnp.where(c * kc + row_iota <= t_idx, sc, NEG)
        b32 = lax.bitcast_convert_type(sc, I32)
        key = b32 ^ (lax.shift_right_arithmetic(b32, 31) & 0x7FFFFFFF)
        key_ref[rows(c), :] = key
        hi_ref[rows(c), :] = lax.shift_right_arithmetic(key, 16).astype(I16)
        lo_ref[rows(c), :] = ((key & 0xFFFF) - 32768).astype(I16)
        return 0

    lax.fori_loop(0, nchunk, score_chunk, 0)

    unroll = math.gcd(COUNT_UNROLL, seq // kc)
    ntrip = (nchunk + unroll - 1) // unroll
    span = unroll * kc

    def fill_neg(c, _):
        hi_ref[rows(c), :] = jnp.full((kc, TQ), NEG_HI, I16)
        lo_ref[rows(c), :] = jnp.full((kc, TQ), NEG_LO, I16)
        return 0

    lax.fori_loop(nchunk, ntrip * unroll, fill_neg, 0)
    n_virtual = (seq - ntrip * span).astype(F32)
    cslab = 64

    def count16(ref, cand):
        def body(t, acc):
            r0 = pl.multiple_of(t * span, span)
            for u in range(unroll):
                v = ref[pl.ds(r0 + u * kc, kc), :]
                hit = jnp.where(v >= cand, jnp.int16(1), jnp.int16(0))
                acc = acc + _tree(jnp.add, [hit[r:r + cslab] for r in range(0, kc, cslab)])
            return acc
        part = lax.fori_loop(0, ntrip, body, jnp.zeros((cslab, TQ), I16))
        return _col_reduce(jnp.add, jnp.sum, part.astype(F32))

    def bisect16(count_fn, cnt_all):
        def step(it, carry):
            u, cnt_u = carry
            cand_u = u | lax.shift_left(jnp.int32(1), jnp.int32(15) - it)
            cnt = count_fn(cand_u - 32768)
            ok = cnt >= n_sel
            return jnp.where(ok, cand_u, u), jnp.where(ok, cnt, cnt_u)
        return lax.fori_loop(0, 16, step, (jnp.zeros((1, TQ), I32), cnt_all))

    def count_hi_ge(cand):
        return count16(hi_ref, cand.astype(I16)) + jnp.where(cand <= NEG_HI, n_virtual, 0.0)

    u_hi, cnt_hi_ge = bisect16(count_hi_ge, jnp.full((1, TQ), float(seq), F32))
    t_hi = u_hi - 32768
    t_hi16 = t_hi.astype(I16)
    def mask_lo(t, _):
        r0 = pl.multiple_of(t * span, span)
        for u in range(unroll):
            r = pl.ds(r0 + u * kc, kc)
            h = hi_ref[r, :]
            lo_ref[r, :] = jnp.where(h == t_hi16, lo_ref[r, :],
                                     jnp.where(h > t_hi16, jnp.int16(I16_MAX), jnp.int16(I16_MIN)))
        return 0

    lax.fori_loop(0, ntrip, mask_lo, 0)
    virt_gt = jnp.where(t_hi < NEG_HI, n_virtual, 0.0)
    virt_lo = jnp.where(t_hi == NEG_HI, n_virtual, 0.0)

    def count_lo_ge(cand):
        return count16(lo_ref, cand.astype(I16)) + virt_gt + jnp.where(cand <= NEG_LO, virt_lo, 0.0)

    u_lo, cnt_thr = bisect16(count_lo_ge, cnt_hi_ge)
    thr = lax.shift_left(t_hi, 16) | u_lo
    excess_ties = jnp.max(cnt_thr) > n_sel

    m_ref[...] = jnp.full(m_ref.shape, M_INIT, F32)
    acc_ref[...] = jnp.zeros(acc_ref.shape, F32)
    eye = jnp.where(lax.broadcasted_iota(I32, (TQ, LANES), 0) == lax.broadcasted_iota(I32, (TQ, LANES), 1),
                    1.0, 0.0).astype(BF16)
    rhs = [jnp.concatenate(
        [jnp.concatenate([qa_ref[:, (A_GROUP * g + h) * A_HEAD_DIM:(A_GROUP * g + h + 1) * A_HEAD_DIM], eye],
                         axis=1) for h in range(A_GROUP)], axis=0) for g in range(hkv)]

    def qk(c, bias_b, g):
        kg = ka_ref[rows(c), g * A_HEAD_DIM:(g + 1) * A_HEAD_DIM]
        return _dot_nt(jnp.concatenate([kg, bias_b], axis=1), rhs[g])

    def attend_chunk(c, bias_b, st, next_first):
        for g in range(hkv):
            if g + 1 < hkv:
                st_next = qk(c, bias_b, g + 1)
            else:
                st_next = next_first() if next_first is not None else None
            m_old = m_ref[g]
            m_new = jnp.maximum(m_old, _col_reduce(jnp.maximum, jnp.max, st))
            alpha = jnp.exp2(m_old - m_new)
            p = jnp.exp2(st - m_new).astype(BF16)
            m_ref[g] = m_new
            vt = vt_ref[c, g * VA_ROWS:(g + 1) * VA_ROWS, :]
            acc_ref[g] = acc_ref[g] * alpha + jnp.dot(vt, p, preferred_element_type=F32)
            st = st_next
        return st

    def plain_bias(c):
        return jnp.where(key_ref[rows(c), :] >= thr, 0.0, NEG).astype(BF16)

    def pv_update(c, g, p, alpha):
        vt = vt_ref[c, g * VA_ROWS:(g + 1) * VA_ROWS, :]
        acc_ref[g] = acc_ref[g] * alpha + jnp.dot(vt, p, preferred_element_type=F32)

    def attend_plain(c, _):
        st, p_prev, alpha_prev = st_ref[...], p_ref[...], al_ref[...]
        c_next = jnp.minimum(c + 1, nchunk - 1)
        c_prev = jnp.maximum(c - 1, 0)
        bias_b = plain_bias(c)
        for g in range(hkv):
            st_next = qk(c, bias_b, g + 1) if g + 1 < hkv else qk(c_next, plain_bias(c_next), 0)
            if g == 0:
                pv_update(c_prev, hkv - 1, p_prev, alpha_prev)
            else:
                pv_update(c, g - 1, p_prev, alpha_prev)
            m_old = m_ref[g]
            m_new = jnp.maximum(m_old, _col_reduce(jnp.maximum, jnp.max, st))
            alpha_prev = jnp.exp2(m_old - m_new)
            p_prev = jnp.exp2(st - m_new).astype(BF16)
            m_ref[g] = m_new
            st = st_next
        st_ref[...] = st
        p_ref[...] = p_prev
        al_ref[...] = alpha_prev
        return 0

    def run_ties():
        def count_gt(c, acc):
            return acc + _col_reduce(jnp.add, jnp.sum, jnp.where(key_ref[rows(c), :] > thr, 1.0, 0.0))
        n_neg_only = (seq - nchunk * kc).astype(F32)
        n_tie = n_sel - (lax.fori_loop(0, nchunk, count_gt, jnp.zeros((1, TQ), F32))
                         + jnp.where(thr < KEY_NEG, n_neg_only, 0.0))

        def attend_ties(c, tie_seen):
            k = key_ref[rows(c), :]
            ltri = jnp.where(lax.broadcasted_iota(I32, (kc, kc), 0) > lax.broadcasted_iota(I32, (kc, kc), 1),
                             1.0, 0.0).astype(BF16)
            eq = jnp.where(k == thr, 1.0, 0.0)
            rank = jnp.dot(ltri, eq.astype(BF16), preferred_element_type=F32) + tie_seen
            keep = jnp.where(k > thr, 1.0, jnp.where(rank < n_tie, eq, 0.0))
            keep = jnp.where(c * kc + row_iota <= t_idx, keep, 0.0)
            bias_b = jnp.where(keep > 0.5, 0.0, NEG).astype(BF16)
            attend_chunk(c, bias_b, qk(c, bias_b, 0), None)
            return tie_seen + _col_reduce(jnp.add, jnp.sum, eq)

        lax.fori_loop(0, nchunk, attend_ties, jnp.zeros((1, TQ), F32))

    def run_plain():
        st_ref[...] = qk(0, plain_bias(0), 0)
        p_ref[...] = jnp.zeros(p_ref.shape, BF16)
        al_ref[...] = jnp.ones(al_ref.shape, F32)
        lax.fori_loop(0, nchunk, attend_plain, 0)
        pv_update(nchunk - 1, hkv - 1, p_ref[...], al_ref[...])

    lax.cond(excess_ties, run_ties, run_plain)

    ssq = jnp.zeros((1, TQ), F32)
    for g in range(hkv):
        o = acc_ref[g, 0:A_HEAD_DIM, :] / acc_ref[g, A_HEAD_DIM:A_HEAD_DIM + 1, :]
        acc_ref[g, 0:A_HEAD_DIM, :] = o
        sq = jnp.sum(o * o, axis=0, keepdims=True)
        for h in range(A_GROUP):
            ssq = ssq + sq[:, h * TQ:(h + 1) * TQ]
    rn = lax.rsqrt(ssq / (hkv * A_GROUP * A_HEAD_DIM) + EPS)
    for g in range(hkv):
        for h in range(A_GROUP):
            col = (A_GROUP * g + h) * A_HEAD_DIM
            oh = acc_ref[g, 0:A_HEAD_DIM, h * TQ:(h + 1) * TQ] * rn
            o_ref[:, col:col + A_HEAD_DIM] = (oh.T * g_ref[:, col:col + A_HEAD_DIM]).astype(o_ref.dtype)


def _dsa(pr, vat3, wt, g_a, lay, seq, n_sel):
    d_a = lay["d_a"]
    hkv = lay["hkv_a"]
    nka = hkv * A_HEAD_DIM
    kc = lay["kc"]
    assert TQ == LANES
    one = pl.Buffered(1)
    kern = functools.partial(_dsa_kernel, seq=seq, n_sel=float(n_sel), hkv=hkv, kc=kc)
    return pl.pallas_call(
        kern,
        grid=(seq // TQ,),
        in_specs=[
            pl.BlockSpec((TQ, d_a), lambda i: (i, 0)),
            pl.BlockSpec((TQ, IDX_HEADS * IDX_DIM), lambda i: (i, lay["off_qi"] // (IDX_HEADS * IDX_DIM))),
            pl.BlockSpec((LANES, TQ), lambda i: (0, i)),
            pl.BlockSpec((seq, nka), lambda i: (0, lay["off_ka"] // nka), pipeline_mode=one),
            pl.BlockSpec((seq, 2 * LANES), lambda i: (0, lay["off_misc"] // (2 * LANES)), pipeline_mode=one),
            pl.BlockSpec((seq // kc, hkv * VA_ROWS, kc), lambda i: (0, 0, 0), pipeline_mode=one),
            pl.BlockSpec((1, d_a), lambda i: (0, 0)),
        ],
        out_specs=pl.BlockSpec((TQ, d_a), lambda i: (i, 0)),
        out_shape=jax.ShapeDtypeStruct((seq, d_a), BF16),
        scratch_shapes=[
            pltpu.VMEM((seq, TQ), I32),
            pltpu.VMEM((seq, TQ), I16),
            pltpu.VMEM((seq, TQ), I16),
            pltpu.VMEM((hkv, VA_ROWS, A_GROUP * TQ), F32),
            pltpu.VMEM((hkv, 1, A_GROUP * TQ), F32),
            pltpu.VMEM((kc, A_GROUP * TQ), F32),
            pltpu.VMEM((kc, A_GROUP * TQ), BF16),
            pltpu.VMEM((1, A_GROUP * TQ), F32),
        ],
        compiler_params=_params(("arbitrary",)),
        name="dsa",
    )(pr, pr, wt, pr, pr, vat3, g_a)


def _swa_kernel(qb_ref, kp_ref, kc_ref, vp_ref, vc_ref, se_ref, so_ref, g_ref, o_ref, ot_ref, *, hkv):
    i = pl.program_id(0)
    npair = B_GROUP // 2
    kwin = jnp.concatenate([kp_ref[...], kc_ref[...]], axis=0).astype(F32)
    r = lax.broadcasted_iota(I32, (2 * WINDOW, TQ), 0) - WINDOW
    q = lax.broadcasted_iota(I32, (2 * WINDOW, TQ), 1)
    ok = (r <= q) & (q - r < WINDOW) & (i * WINDOW + r >= 0)
    bias_b = jnp.where(ok, 0.0, NEG).astype(BF16)
    eye = jnp.where(lax.broadcasted_iota(I32, (TQ, LANES), 0) == lax.broadcasted_iota(I32, (TQ, LANES), 1),
                    1.0, 0.0).astype(BF16)
    ones_rows = jnp.where(lax.broadcasted_iota(I32, (VA_PAD, 2 * WINDOW), 0) == 0, 1.0, 0.0).astype(BF16)
    lane = lax.broadcasted_iota(I32, (2 * WINDOW, LANES), 1)
    nq = npair * TQ
    ssq = jnp.zeros((1, TQ), F32)
    for g in range(hkv):
        kcol = kwin[:, (g // 2) * LANES:(g // 2 + 1) * LANES]
        rolled = pltpu.roll(kcol, 64, 1)
        lo_src, hi_src = (kcol, rolled) if g % 2 == 0 else (rolled, kcol)
        k_even = jnp.where(lane < 64, lo_src, 0.0).astype(BF16)
        k_odd = jnp.where(lane >= 64, hi_src, 0.0).astype(BF16)
        lhs = jnp.concatenate([jnp.concatenate([k_even, bias_b], axis=1),
                               jnp.concatenate([k_odd, bias_b], axis=1)], axis=0)
        rhs = jnp.concatenate(
            [jnp.concatenate([qb_ref[:, (npair * g + p) * LANES:(npair * g + p + 1) * LANES], eye], axis=1)
             for p in range(npair)], axis=0)
        st = _dot_nt(lhs, rhs)
        ps, ms = [], []
        for half, s_ref in ((0, se_ref), (1, so_ref)):
            sth = st[half * 2 * WINDOW:(half + 1) * 2 * WINDOW]
            m = jnp.maximum(_col_reduce(jnp.maximum, jnp.max, sth), s_ref[g])
            ps.append(jnp.exp2(sth - m).astype(BF16))
            ms.append(m)
        v_aug = jnp.concatenate(
            [jnp.concatenate([vp_ref[g * B_HEAD_DIM:(g + 1) * B_HEAD_DIM, :],
                              vc_ref[g * B_HEAD_DIM:(g + 1) * B_HEAD_DIM, :]], axis=1), ones_rows], axis=0)
        ot = jnp.dot(v_aug, jnp.concatenate(ps, axis=1), preferred_element_type=F32)
        outs = []
        for half, s_ref in ((0, se_ref), (1, so_ref)):
            den = ot[B_HEAD_DIM:B_HEAD_DIM + 1, half * nq:(half + 1) * nq] + jnp.exp2(s_ref[g] - ms[half])
            outs.append(ot[0:B_HEAD_DIM, half * nq:(half + 1) * nq] / den)
        for p in range(npair):
            blk = jnp.concatenate([outs[0][:, p * TQ:(p + 1) * TQ], outs[1][:, p * TQ:(p + 1) * TQ]],
                                  axis=0)
            ot_ref[npair * g + p] = blk
            ssq = ssq + jnp.sum(blk * blk, axis=0, keepdims=True)
    rn = lax.rsqrt(ssq / (hkv * B_GROUP * B_HEAD_DIM) + EPS)
    for c in range(hkv * npair):
        o_ref[:, c * LANES:(c + 1) * LANES] = (
            (ot_ref[c] * rn).T * g_ref[:, c * LANES:(c + 1) * LANES]).astype(o_ref.dtype)


def _swa(pr, vbt3, sink_e, sink_o, g_b, lay, seq):
    d_b = lay["d_b"]
    hkv = lay["hkv_b"]
    kc = lay["kc"]
    per = kc // TQ
    nvb = hkv * B_HEAD_DIM
    off_kb = lay["off_misc"] + 2 * LANES
    assert nvb % LANES == 0 and off_kb % nvb == 0
    col_kb = off_kb // nvb
    prev = lambda i: jnp.maximum(i - 1, 0)
    kern = functools.partial(_swa_kernel, hkv=hkv)
    return pl.pallas_call(
        kern,
        grid=(seq // TQ,),
        in_specs=[
            pl.BlockSpec((TQ, d_b), lambda i: (i, lay["off_qb"] // d_b)),
            pl.BlockSpec((TQ, nvb), lambda i: (prev(i), col_kb)),
            pl.BlockSpec((TQ, nvb), lambda i: (i, col_kb)),
            pl.BlockSpec((None, nvb, TQ), lambda i: (prev(i) // per, 0, prev(i) % per)),
            pl.BlockSpec((None, nvb, TQ), lambda i: (i // per, 0, i % per)),
            pl.BlockSpec((hkv, 1, (B_GROUP // 2) * TQ), lambda i: (0, 0, 0)),
            pl.BlockSpec((hkv, 1, (B_GROUP // 2) * TQ), lambda i: (0, 0, 0)),
            pl.BlockSpec((1, d_b), lambda i: (0, 0)),
        ],
        out_specs=pl.BlockSpec((TQ, d_b), lambda i: (i, 0)),
        out_shape=jax.ShapeDtypeStruct((seq, d_b), BF16),
        scratch_shapes=[pltpu.VMEM((hkv * B_GROUP // 2, LANES, TQ), F32)],
        compiler_params=_params(("parallel",)),
        name="swa",
    )(pr, pr, pr, vbt3, vbt3, sink_e, sink_o, g_b)


def _oproj_kernel(na_ref, nb_ref, wa_ref, wb_ref, x_ref, gt_ref, o_ref):
    acc = jnp.dot(na_ref[...], wa_ref[...], preferred_element_type=F32)
    acc = acc + jnp.dot(nb_ref[...], wb_ref[...], preferred_element_type=F32)
    o_ref[...] = x_ref[...] + gt_ref[...] * acc


def _oproj(na, nb, w_out, layer, x, gate):
    s, d_a = na.shape
    d_b = nb.shape[1]
    d = x.shape[1]
    assert d_a == d_b
    tm = _pick(s, (1024, 512, 256, 128))
    tn = _pick(d, (1024, 512, 256, 128))
    return pl.pallas_call(
        _oproj_kernel,
        grid=(s // tm, d // tn),
        in_specs=[
            pl.BlockSpec((tm, d_a), lambda i, j: (i, 0)),
            pl.BlockSpec((tm, d_b), lambda i, j: (i, 0)),
            pl.BlockSpec((None, d_a, tn), lambda i, j: (layer, 0, j)),
            pl.BlockSpec((None, d_b, tn), lambda i, j: (layer, 1, j)),
            pl.BlockSpec((tm, tn), lambda i, j: (i, j)),
            pl.BlockSpec((1, tn), lambda i, j: (0, j)),
        ],
        out_specs=pl.BlockSpec((tm, tn), lambda i, j: (i, j)),
        out_shape=jax.ShapeDtypeStruct((s, d), F32),
        compiler_params=_params(("parallel", "arbitrary")),
        name="oproj",
    )(na, nb, w_out, w_out, x, gate)


HALO = 8
UP_PARTS = 1
FF_ALIGN = 512


def _up_kernel(h_ref, wg_ref, wv_ref, cwg_ref, cwv_ref, cbg_ref, cbv_ref, o_ref, eg_ref, ev_ref, *, tm, parts):
    i = pl.program_id(1)

    @pl.when(i == 0)
    def _():
        eg_ref[0:HALO, :] = jnp.zeros((HALO, eg_ref.shape[1]), F32)
        ev_ref[0:HALO, :] = jnp.zeros((HALO, ev_ref.shape[1]), F32)

    @pl.when(i > 0)
    def _():
        eg_ref[0:HALO, :] = eg_ref[tm:tm + HALO, :]
        ev_ref[0:HALO, :] = ev_ref[tm:tm + HALO, :]

    def conv(e_ref, cw_ref, cb_ref, r0, n):
        y = cb_ref[...] + cw_ref[CONV_WIDTH - 1:CONV_WIDTH, :] * e_ref[HALO + r0:HALO + r0 + n, :]
        for k in range(1, CONV_WIDTH):
            y = y + cw_ref[CONV_WIDTH - 1 - k:CONV_WIDTH - k, :] * e_ref[HALO + r0 - k:HALO + r0 - k + n, :]
        return y

    n = tm // parts
    for p in range(parts):
        h = h_ref[p * n:(p + 1) * n, :]
        eg_ref[HALO + p * n:HALO + (p + 1) * n, :] = jnp.dot(h, wg_ref[...], preferred_element_type=F32)
        ev_ref[HALO + p * n:HALO + (p + 1) * n, :] = jnp.dot(h, wv_ref[...], preferred_element_type=F32)
    for p in range(parts):
        gte = conv(eg_ref, cwg_ref, cbg_ref, p * n, n)
        val = conv(ev_ref, cwv_ref, cbv_ref, p * n, n)
        o_ref[p * n:(p + 1) * n, :] = (gte * (1.0 / (1.0 + jnp.exp(-gte))) * val).astype(o_ref.dtype)


def _up(h, w_up, layer, conv_w, conv_b):
    s, d = h.shape
    f = w_up.shape[2] // 2
    tn = _pick(f, (512, 256, 128))
    tm = _pick(s, (1024, 512, 256, 128))
    nf = f // tn
    kern = functools.partial(_up_kernel, tm=tm, parts=UP_PARTS if tm % (UP_PARTS * 128) == 0 else 1)
    gate = lambda j, i: (layer, 0, j)
    val = lambda j, i: (layer, 0, nf + j)
    return pl.pallas_call(
        kern,
        grid=(nf, s // tm),
        in_specs=[
            pl.BlockSpec((tm, d), lambda j, i: (i, 0)),
            pl.BlockSpec((None, d, tn), gate), pl.BlockSpec((None, d, tn), val),
            pl.BlockSpec((None, CONV_WIDTH, tn), gate), pl.BlockSpec((None, CONV_WIDTH, tn), val),
            pl.BlockSpec((None, 1, tn), gate), pl.BlockSpec((None, 1, tn), val),
        ],
        out_specs=pl.BlockSpec((tm, tn), lambda j, i: (i, j)),
        out_shape=jax.ShapeDtypeStruct((s, f), BF16),
        scratch_shapes=[pltpu.VMEM((tm + HALO, tn), F32), pltpu.VMEM((tm + HALO, tn), F32)],
        compiler_params=_params(("arbitrary", "arbitrary")),
        name="up_conv_gate",
    )(h, w_up, w_up, conv_w, conv_w, conv_b, conv_b)


def _down_kernel(a_ref, w_ref, x_ref, gt_ref, o_ref):
    o_ref[...] = x_ref[...] + gt_ref[...] * jnp.dot(a_ref[...], w_ref[...], preferred_element_type=F32)


def _down(act, wd, layer, x, gate):
    s, f = act.shape
    d = x.shape[1]
    tm = _pick(s, (512, 256, 128))
    tn = _pick(d, (512, 256, 128))
    return pl.pallas_call(
        _down_kernel,
        grid=(s // tm, d // tn),
        in_specs=[
            pl.BlockSpec((tm, f), lambda i, j: (i, 0)),
            pl.BlockSpec((None, f, tn), lambda i, j: (layer, 0, j)),
            pl.BlockSpec((tm, tn), lambda i, j: (i, j)),
            pl.BlockSpec((1, tn), lambda i, j: (0, j)),
        ],
        out_specs=pl.BlockSpec((tm, tn), lambda i, j: (i, j)),
        out_shape=jax.ShapeDtypeStruct((s, d), F32),
        compiler_params=_params(("parallel", "arbitrary")),
        name="down",
    )(act, wd, x, gate)


def _cast_pad_kernel(x_ref, o_ref, *, axis, n_real, n_out):
    r = pl.program_id(axis) % n_out

    @pl.when(r < n_real)
    def _():
        o_ref[...] = x_ref[...].astype(o_ref.dtype)

    @pl.when(r >= n_real)
    def _():
        o_ref[...] = jnp.zeros(o_ref.shape, o_ref.dtype)


def _pad_unit(f, padf):
    g = np.gcd(f, padf) if padf else f
    return _pick(int(g), (512, 256, 128))


def _cast_up(w_up, f, padf):
    depth, d, _ = w_up.shape
    cw = _pad_unit(f, padf)
    n_real, n_out = f // cw, (f + padf) // cw
    src = lambda l, t: (l, 0, (t // n_out) * n_real + jnp.minimum(t % n_out, n_real - 1))
    kern = functools.partial(_cast_pad_kernel, axis=1, n_real=n_real, n_out=n_out)
    return pl.pallas_call(
        kern,
        grid=(depth, 2 * n_out),
        in_specs=[pl.BlockSpec((None, d, cw), src)],
        out_specs=pl.BlockSpec((None, d, cw), lambda l, t: (l, 0, t)),
        out_shape=jax.ShapeDtypeStruct((depth, d, 2 * (f + padf)), BF16),
        compiler_params=_params(("parallel", "arbitrary")),
        name="cast_up",
    )(w_up)


def _cast_down(w_down, padf):
    depth, f, d = w_down.shape
    rt = _pad_unit(f, padf)
    n_real, n_out = f // rt, (f + padf) // rt
    kern = functools.partial(_cast_pad_kernel, axis=1, n_real=n_real, n_out=n_out)
    return pl.pallas_call(
        kern,
        grid=(depth, n_out),
        in_specs=[pl.BlockSpec((None, rt, d), lambda l, t: (l, jnp.minimum(t, n_real - 1), 0))],
        out_specs=pl.BlockSpec((None, rt, d), lambda l, t: (l, t, 0)),
        out_shape=jax.ShapeDtypeStruct((depth, f + padf, d), BF16),
        compiler_params=_params(("parallel", "arbitrary")),
        name="cast_down",
    )(w_down)


def _layout(d_model, seq):
    d_a = d_model // 2
    d_b = d_model - d_a
    ha = d_a // A_HEAD_DIM
    hkv_a = ha // A_GROUP
    hb = d_b // B_HEAD_DIM
    hkv_b = hb // B_GROUP
    n_qi = IDX_HEADS * IDX_DIM
    n_ka = hkv_a * A_HEAD_DIM
    n_kb = hkv_b * B_HEAD_DIM
    misc_raw = 2 * LANES + n_kb
    seg_w = [d_a, d_b, n_qi, n_ka]
    tn = 512
    while any(w % tn for w in seg_w) or tn > misc_raw + LANES:
        tn //= 2
    misc_w = -(-misc_raw // tn) * tn
    off_qb = d_a
    off_qi = off_qb + d_b
    off_ka = off_qi + n_qi
    off_misc = off_ka + n_ka
    n_r = off_misc + misc_w
    bounds = (off_qb // tn, off_qi // tn, off_ka // tn, off_misc // tn)
    scales = (A_HEAD_DIM ** -0.5 * LOG2E, B_HEAD_DIM ** -0.5 * LOG2E, IDX_DIM ** -0.5)
    kc = min(KC, seq)
    assert d_a == d_b and seq % kc == 0 and kc % TQ == 0 and off_qi % n_qi == 0 and off_ka % n_ka == 0
    assert off_misc % (2 * LANES) == 0
    return dict(d_a=d_a, d_b=d_b, ha=ha, hkv_a=hkv_a, hb=hb, hkv_b=hkv_b, n_qi=n_qi, n_ka=n_ka, n_kb=n_kb,
                tn=tn, misc_w=misc_w, off_qb=off_qb, off_qi=off_qi, off_ka=off_ka, off_misc=off_misc,
                n_r=n_r, bounds=bounds, scales=scales, kc=kc, n_va=n_ka, n_vb=n_kb)


M_COPY, M_KI_EVEN, M_KI_ODD, M_ZERO, M_WI = range(5)
ROW_ALIGN = 16


def _relayout_kernel(off, mode, a_ref, o_ref):
    m = mode[pl.program_id(1)]
    lane = lax.broadcasted_iota(I32, o_ref.shape, 1)

    def put(fn):
        def _():
            o_ref[...] = fn(a_ref[0].T).astype(o_ref.dtype)
        return _

    pl.when(m == M_COPY)(put(lambda x: x))
    pl.when(m == M_KI_EVEN)(put(lambda x: jnp.where(lane < IDX_DIM, x, 0.0)))
    pl.when(m == M_KI_ODD)(put(lambda x: jnp.where(lane >= IDX_DIM, pltpu.roll(x, IDX_DIM, 1), 0.0)))
    pl.when(m == M_ZERO)(put(lambda x: jnp.zeros_like(x)))
    pl.when(m == M_WI)(put(lambda x: jnp.where(lane < IDX_HEADS, pltpu.roll(x, LANES - IDX_DIM, 1), 0.0)))


def _relayout(w_in_t, tiles):
    depth, n_src, d = w_in_t.shape
    offs = [col for _, col in tiles]
    assert all(o % ROW_ALIGN == 0 and o + LANES <= n_src for o in offs)
    as_i32 = lambda v: jnp.asarray(np.asarray(v, np.int32))
    n_tiles = len(tiles)
    window = pl.BlockSpec((pl.Element(1), pl.Element(LANES), pl.Element(d)),
                          lambda l, t, off, m: (l, pl.multiple_of(off[t], ROW_ALIGN), 0))
    return pl.pallas_call(
        _relayout_kernel,
        grid_spec=pltpu.PrefetchScalarGridSpec(
            num_scalar_prefetch=2,
            grid=(depth, n_tiles),
            in_specs=[window],
            out_specs=pl.BlockSpec((None, d, LANES), lambda l, t, off, m: (l, 0, t)),
        ),
        out_shape=jax.ShapeDtypeStruct((depth, d, n_tiles * LANES), BF16),
        compiler_params=_params(("parallel", "arbitrary")),
        name="relayout_w_in",
    )(as_i32(offs), as_i32([m for m, _ in tiles]), w_in_t)


def _prep_w_in(w_in, lay):
    d_a, d_b = lay["d_a"], lay["d_b"]
    widths = (d_a, lay["n_ka"], lay["n_ka"], lay["n_qi"], IDX_DIM, IDX_HEADS, d_b, lay["n_kb"], lay["n_kb"])
    o_qa, o_ka, o_va, o_qi, o_ki, o_wi, o_qb, o_kb, o_vb = [int(v) for v in np.cumsum((0,) + widths[:-1])]
    assert o_wi == o_ki + IDX_DIM and 2 * IDX_DIM == LANES
    seg = lambda off, width: [(M_COPY, off + k * LANES) for k in range(width // LANES)]
    pad_m = lay["misc_w"] - (2 * LANES + lay["n_kb"])
    w_in_t = jnp.swapaxes(w_in, 1, 2)
    w_r = _relayout(w_in_t, seg(o_qa, d_a) + seg(o_qb, d_b) + seg(o_qi, lay["n_qi"]) + seg(o_ka, lay["n_ka"])
                    + [(M_KI_EVEN, o_ki), (M_KI_ODD, o_ki)] + seg(o_kb, lay["n_kb"])
                    + [(M_ZERO, 0)] * (pad_m // LANES))
    w_v = _relayout(w_in_t, seg(o_va, lay["n_va"]) + seg(o_vb, lay["n_vb"]) + [(M_WI, o_ki)])
    return w_r, w_v


def _rope_tabs(positions):
    pos = positions[0].astype(F32)
    inv_a = ROPE_THETA ** (-jnp.arange(0, A_HEAD_DIM, 2, dtype=F32) / A_HEAD_DIM)
    ang_a = pos[:, None] * inv_a
    ca, sa = jnp.cos(ang_a), jnp.sin(ang_a)
    inv_6 = ROPE_THETA ** (-jnp.arange(0, B_HEAD_DIM, 2, dtype=F32) / B_HEAD_DIM)
    ang_6 = pos[:, None] * inv_6
    c6, s6 = jnp.cos(ang_6), jnp.sin(ang_6)
    z6 = jnp.zeros_like(s6)
    return jnp.concatenate([
        ca, ca, -sa, sa,
        c6, c6, c6, c6,
        z6, s6, z6, s6,
        -s6, z6, -s6, z6], axis=-1)


def kernel(x, c, positions, w_ada, b_ada, g_mix, w_in, g_out_a, g_out_b, sinks, w_out, g_ffn, w_up,
           conv_w, conv_b, w_down, g_final):
    b, seq, d = x.shape
    assert b == 1 and IDX_DIM == B_HEAD_DIM
    depth = w_ada.shape[0]
    lay = _layout(d, seq)
    n_sel = min(TOPK_MAX, seq // 4)
    hkv_b = lay["hkv_b"]

    w_r, w_v = _prep_w_in(w_in, lay)
    w_out_b = w_out.astype(BF16)
    f = w_down.shape[1]
    padf = -f % FF_ALIGN
    halves = lambda a: jnp.concatenate(
        [a[..., :f], jnp.zeros(a.shape[:-1] + (padf,), a.dtype), a[..., f:],
         jnp.zeros(a.shape[:-1] + (padf,), a.dtype)], axis=-1)
    w_up_b = _cast_up(w_up, f, padf)
    w_down_b = _cast_down(w_down, padf)
    conv_wp = halves(conv_w)
    conv_b3 = halves(conv_b)[:, None, :]
    tabs = _rope_tabs(positions)
    sk = (sinks * LOG2E).reshape(depth, hkv_b, B_GROUP // 2, 2)
    rep = lambda a: jnp.repeat(a, TQ, axis=-1).reshape(depth, hkv_b, 1, (B_GROUP // 2) * TQ)
    sink_e, sink_o = rep(sk[..., 0]), rep(sk[..., 1])

    mod = _ada(c.reshape(d, 1), w_ada, b_ada)
    xs = x[0]
    for l in range(depth):
        sh_m, sc_m, gt_m, sh_f, sc_f, gt_f = [mod[l, :, n * d:(n + 1) * d] for n in range(N_MOD)]
        h = _norm_mod(xs, g_mix[l][None, :], sc_m, sh_m)
        pr = _proj_rope(h, w_r, l, tabs, lay)
        vat3, vbt3, wt = _proj_v(h, w_v, l, lay["n_va"], lay["n_vb"], lay["kc"])
        na = _dsa(pr, vat3, wt, g_out_a[l][None, :], lay, seq, n_sel)
        nb = _swa(pr, vbt3, sink_e[l], sink_o[l], g_out_b[l][None, :], lay, seq)
        xs = _oproj(na, nb, w_out_b, l, xs, gt_m)
        h = _norm_mod(xs, g_ffn[l][None, :], sc_f, sh_f)
        act = _up(h, w_up_b, l, conv_wp, conv_b3)
        xs = _down(act, w_down_b, l, xs, gt_f)
    return _final_norm(xs, g_final[None, :])[None]
```
